```python
import math
import jax, jax.numpy as jnp
from jax import lax
import numpy as np

D_MODEL = 2048
BATCH = 2
SEQ = 4096
DEPTH = 1
DEC_BATCH = 128
DEC_SEQ = 1
PAST_LEN = 8192
PAGE_SIZE = 128

MLA_HEADS = 12
MLA_NOPE_DIM = 128
MLA_ROPE_DIM = 64
MLA_V_DIM = 128
Q_LORA_RANK = 512
KV_LORA_RANK = 256
MLA_WIDTH = MLA_HEADS * MLA_V_DIM
SB_HEADS = 4
SB_HEAD_DIM = 128
SB_WIDTH = SB_HEADS * SB_HEAD_DIM
D_MIX = MLA_WIDTH + SB_WIDTH
IN_SIZES = (Q_LORA_RANK, KV_LORA_RANK, MLA_ROPE_DIM, MLA_WIDTH,
            SB_WIDTH, SB_WIDTH, SB_WIDTH, SB_WIDTH)
D_IN = Q_LORA_RANK + KV_LORA_RANK + MLA_ROPE_DIM + MLA_WIDTH + 4 * SB_WIDTH
QBLOCK = 128
ROPE_BASE = 10000.0
EPS = 1e-6
MASK_VALUE = -1e30

kernel_name = "hymba_mla_stickbreaking_decode_step"


def _rmsnorm(x, g):
    xf = x.astype(jnp.float32)
    y = xf * lax.rsqrt(jnp.mean(xf * xf, axis=-1, keepdims=True) + EPS)
    return (y * g.astype(jnp.float32)).astype(x.dtype)


def _rope(x, pos):
    half = MLA_ROPE_DIM // 2
    inv_freq = ROPE_BASE ** (-jnp.arange(half, dtype=jnp.float32) / half)
    ang = pos.astype(jnp.float32)[:, None] * inv_freq[None, :]
    ang = ang.reshape(ang.shape[0], *([1] * (x.ndim - 3)), half)
    cos, sin = jnp.cos(ang), jnp.sin(ang)
    xf = x.astype(jnp.float32)
    x1, x2 = xf[..., :half], xf[..., half:]
    return jnp.concatenate([x1 * cos - x2 * sin, x1 * sin + x2 * cos], axis=-1).astype(x.dtype)


def _sweep_query_blocks(fn, qs, q_pos):
    sq = q_pos.shape[0]
    bq = min(QBLOCK, sq)
    nb = -(-sq // bq)
    pad = nb * bq - sq
    qs_b = tuple(
        jnp.pad(q, [(0, 0), (0, pad)] + [(0, 0)] * (q.ndim - 2))
        .reshape(q.shape[0], nb, bq, *q.shape[2:]).swapaxes(0, 1)
        for q in qs)
    pos_b = jnp.pad(q_pos, (0, pad), mode="edge").reshape(nb, bq)
    out = lax.map(fn, (*qs_b, pos_b))
    out = out.swapaxes(0, 1)
    return out.reshape(out.shape[0], nb * bq, *out.shape[3:])[:, :sq]


def _mla_attend(q_nope, q_pe, q_pos, c_kv, k_pe, k_pos, w_kv_b):
    w_uk = w_kv_b[..., :MLA_NOPE_DIM]
    w_uv = w_kv_b[..., MLA_NOPE_DIM:]
    q_abs = jnp.einsum('bqhd,chd->bqhc', q_nope, w_uk)
    scale = 1.0 / math.sqrt(MLA_NOPE_DIM + MLA_ROPE_DIM)

    def block(args):
        qa, qp, qpos = args
        s = (jnp.einsum('bqhc,bkc->bhqk', qa, c_kv)
             + jnp.einsum('bqhr,bkr->bhqk', qp, k_pe)).astype(jnp.float32) * scale
        mask = k_pos[None, :] <= qpos[:, None]
        s = jnp.where(mask, s, MASK_VALUE)
        p = jax.nn.softmax(s, axis=-1).astype(c_kv.dtype)
        return jnp.einsum('bhqk,bkc->bqhc', p, c_kv)

    out_lat = _sweep_query_blocks(block, (q_abs, q_pe), q_pos)
    out = jnp.einsum('bqhc,chd->bqhd', out_lat, w_uv)
    return out.reshape(out.shape[0], out.shape[1], MLA_WIDTH)


def _sb_attend(q, k, v, q_pos, k_pos):
    inv_sqrt_d = 1.0 / math.sqrt(SB_HEAD_DIM)

    def block(args):
        qb, qpos = args
        z = jnp.einsum('bqhd,bkhd->bhqk', qb, k).astype(jnp.float32) * inv_sqrt_d
        mask = k_pos[None, :] < qpos[:, None]
        log_1m = jnp.where(mask, jax.nn.log_sigmoid(-z), 0.0)
        tail = lax.cumsum(log_1m, axis=3, reverse=True) - log_1m
        w = jnp.where(mask, jnp.exp(jax.nn.log_sigmoid(z) + tail), 0.0).astype(v.dtype)
        return jnp.einsum('bhqk,bkhd->bqhd', w, v)

    out = _sweep_query_blocks(block, (q,), q_pos)
    return out.reshape(out.shape[0], out.shape[1], SB_WIDTH)


def _layer(x, q_pos, past, w_in, g_q_a, w_q_b, g_kv_a, w_kv_b, w_out, g_pre, g_post):
    b, s, _ = x.shape
    h = _rmsnorm(x, g_pre)
    z = jnp.einsum('bsd,de->bse', h, w_in)
    offs = np.cumsum(IN_SIZES)[:-1].tolist()
    c_q, c_kv, k_pe, gate_mla, sb_q, sb_k, sb_v, gate_sb = jnp.split(z, offs, axis=-1)
    q = jnp.einsum('bsr,re->bse', _rmsnorm(c_q, g_q_a), w_q_b)
    q = q.reshape(b, s, MLA_HEADS, MLA_NOPE_DIM + MLA_ROPE_DIM)
    q_nope = q[..., :MLA_NOPE_DIM]
    q_pe = _rope(q[..., MLA_NOPE_DIM:], q_pos)
    c_kv = _rmsnorm(c_kv, g_kv_a)
    k_pe = _rope(k_pe, q_pos)
    sb_q = sb_q.reshape(b, s, SB_HEADS, SB_HEAD_DIM)
    sb_k = sb_k.reshape(b, s, SB_HEADS, SB_HEAD_DIM)
    sb_v = sb_v.reshape(b, s, SB_HEADS, SB_HEAD_DIM)

    if past is None:
        lat_all, kpe_all, k_all, v_all, k_pos = c_kv, k_pe, sb_k, sb_v, q_pos
    else:
        p_lat, p_kpe, p_k, p_v = past
        lat_all = jnp.concatenate([p_lat, c_kv], axis=1)
        kpe_all = jnp.concatenate([p_kpe, k_pe], axis=1)
        k_all = jnp.concatenate([p_k, sb_k], axis=1)
        v_all = jnp.concatenate([p_v, sb_v], axis=1)
        k_pos = jnp.arange(p_lat.shape[1] + s, dtype=jnp.int32)

    o_mla = _mla_attend(q_nope, q_pe, q_pos, lat_all, kpe_all, k_pos, w_kv_b)
    o_sb = _sb_attend(sb_q, k_all, v_all, q_pos, k_pos)
    mix = jnp.concatenate([o_mla * jax.nn.silu(gate_mla), o_sb * jax.nn.silu(gate_sb)], axis=-1)
    y = _rmsnorm(jnp.einsum('bse,ed->bsd', mix, w_out), g_post)
    return x + y, (c_kv, k_pe, sb_k, sb_v)


def setup_inputs(seed: int = 0) -> dict:
    key = jax.random.key(seed)
    ks = jax.random.split(key, 20)
    n_pages = PAST_LEN // PAGE_SIZE
    n_used = DEC_BATCH * n_pages
    n_pool = (n_used * 5 + 3) // 4
    f32 = jnp.float32

    def nrm(k, shape, scale=1.0):
        return jax.random.normal(k, shape, f32) * scale

    x_prompt = nrm(ks[0], (BATCH, SEQ, D_MODEL))
    x_sample = nrm(ks[1], (DEC_BATCH, DEC_SEQ, D_MODEL))
    cache_mla_latent = nrm(ks[2], (DEPTH, n_pool, PAGE_SIZE, KV_LORA_RANK))
    cache_mla_krope = nrm(ks[3], (DEPTH, n_pool, PAGE_SIZE, MLA_ROPE_DIM))
    cache_sb_k = nrm(ks[4], (DEPTH, n_pool, PAGE_SIZE, SB_HEADS, SB_HEAD_DIM))
    cache_sb_v = nrm(ks[5], (DEPTH, n_pool, PAGE_SIZE, SB_HEADS, SB_HEAD_DIM))
    page_table = jax.random.permutation(ks[6], n_pool)[:n_used].reshape(DEC_BATCH, n_pages).astype(jnp.int32)
    w_in = nrm(ks[7], (DEPTH, D_MODEL, D_IN), D_MODEL ** -0.5)
    g_q_a = 1.0 + nrm(ks[8], (DEPTH, Q_LORA_RANK), 0.02)
    w_q_b = nrm(ks[9], (DEPTH, Q_LORA_RANK, MLA_HEADS * (MLA_NOPE_DIM + MLA_ROPE_DIM)), Q_LORA_RANK ** -0.5)
    g_kv_a = 1.0 + nrm(ks[10], (DEPTH, KV_LORA_RANK), 0.02)
    w_kv_b = nrm(ks[11], (DEPTH, KV_LORA_RANK, MLA_HEADS, MLA_NOPE_DIM + MLA_V_DIM), KV_LORA_RANK ** -0.5)
    w_out = nrm(ks[12], (DEPTH, D_MIX, D_MODEL), D_MIX ** -0.5)
    g_pre = 1.0 + nrm(ks[13], (DEPTH, D_MODEL), 0.02)
    g_post = 1.0 + nrm(ks[14], (DEPTH, D_MODEL), 0.02)
    return {"x_prompt": x_prompt, "x_sample": x_sample,
            "cache_mla_latent": cache_mla_latent, "cache_mla_krope": cache_mla_krope,
            "cache_sb_k": cache_sb_k, "cache_sb_v": cache_sb_v, "page_table": page_table,
            "w_in": w_in, "g_q_a": g_q_a, "w_q_b": w_q_b, "g_kv_a": g_kv_a,
            "w_kv_b": w_kv_b, "w_out": w_out, "g_pre": g_pre, "g_post": g_post}


def reference(x_prompt, x_sample, cache_mla_latent, cache_mla_krope, cache_sb_k, cache_sb_v,
              page_table, w_in, g_q_a, w_q_b, g_kv_a, w_kv_b, w_out, g_pre, g_post):
    seq = x_prompt.shape[1]
    dec_batch, dec_seq = x_sample.shape[0], x_sample.shape[1]
    past_len = page_table.shape[1] * cache_mla_latent.shape[2]
    pos_prompt = jnp.arange(seq, dtype=jnp.int32)
    pos_sample = past_len + jnp.arange(dec_seq, dtype=jnp.int32)

    def gather(cache):
        rows = jnp.take(cache, page_table, axis=0)
        return rows.reshape(dec_batch, past_len, *cache.shape[2:])

    xp, xs = x_prompt, x_sample
    p_lat, p_kpe, p_k, p_v = [], [], [], []
    s_lat, s_kpe, s_k, s_v = [], [], [], []
    for l in range(DEPTH):
        weights = (w_in[l], g_q_a[l], w_q_b[l], g_kv_a[l], w_kv_b[l], w_out[l], g_pre[l], g_post[l])
        xp, new_p = _layer(xp, pos_prompt, None, *weights)
        past = (gather(cache_mla_latent[l]), gather(cache_mla_krope[l]),
                gather(cache_sb_k[l]), gather(cache_sb_v[l]))
        xs, new_s = _layer(xs, pos_sample, past, *weights)
        p_lat.append(new_p[0]); p_kpe.append(new_p[1]); p_k.append(new_p[2]); p_v.append(new_p[3])
        s_lat.append(new_s[0]); s_kpe.append(new_s[1]); s_k.append(new_s[2]); s_v.append(new_s[3])

    return (xp, xs,
            jnp.stack(p_lat), jnp.stack(p_kpe), jnp.stack(p_k), jnp.stack(p_v),
            jnp.stack(s_lat), jnp.stack(s_kpe), jnp.stack(s_k), jnp.stack(s_v))
```

```python
import functools
import math

import jax
import jax.numpy as jnp
from jax import lax
from jax.experimental import pallas as pl
from jax.experimental.pallas import tpu as pltpu

F32 = jnp.float32
BF16 = jnp.bfloat16

D_MODEL = 2048
MLA_HEADS = 12
MLA_NOPE_DIM = 128
MLA_ROPE_DIM = 64
MLA_V_DIM = 128
Q_LORA_RANK = 512
KV_LORA_RANK = 256
MLA_WIDTH = MLA_HEADS * MLA_V_DIM
SB_HEADS = 4
SB_HEAD_DIM = 128
SB_WIDTH = SB_HEADS * SB_HEAD_DIM
ROPE_BASE = 10000.0
EPS = 1e-6
MASK_VALUE = -1e30

LANE = 128
ROPE_PAD = LANE
QK_DIM = KV_LORA_RANK + ROPE_PAD
HEADS_PAD = 16
VMEM_LIMIT = 56 * 1024 * 1024

_O_CQ = 0
_O_CKV = _O_CQ + Q_LORA_RANK
_O_KA = _O_CKV + KV_LORA_RANK
_O_KB = _O_KA + ROPE_PAD
_O_GM = _O_KB + ROPE_PAD
_O_SQ = _O_GM + MLA_WIDTH
_O_SK = _O_SQ + SB_WIDTH
_O_SV = _O_SK + SB_WIDTH
_O_GS = _O_SV + SB_WIDTH
_W_IN_EXT = _O_GS + SB_WIDTH

MLA_SCALE = 1.0 / math.sqrt(MLA_NOPE_DIM + MLA_ROPE_DIM)
SB_SCALE = 1.0 / math.sqrt(SB_HEAD_DIM)


def _params(sem):
    return pltpu.CompilerParams(dimension_semantics=sem, vmem_limit_bytes=VMEM_LIMIT)


def _const_spec(shape):
    nd = len(shape)
    return pl.BlockSpec(shape, lambda *_: (0,) * nd, pipeline_mode=pl.Buffered(1))


def _rms(x, g):
    return x * lax.rsqrt(jnp.mean(x * x, axis=-1, keepdims=True) + EPS) * g


def _silu(x):
    return x / (1.0 + jnp.exp(-x))


def _dot(a, b):
    return jnp.dot(a, b, preferred_element_type=F32)


def _dot_nt(a, b):
    return lax.dot_general(a, b, (((1,), (1,)), ((), ())), preferred_element_type=F32)


def _proj_in_kernel(x_ref, gpre_ref, win_ref, gq_ref, gkv_ref, wq_ref, wuk_ref, invf_ref, sgn_ref,
                    lat_ref, kpe_ref, kcat_ref, gmla_ref, sbq_ref, sbk_ref, sbv_ref,
                    sbkb_ref, sbvb_ref, gsb_ref, qcat_ref, *, tm, pos0, pos_stride):
    i = pl.program_id(1)
    h = _rms(x_ref[0], gpre_ref[...]).astype(BF16)

    def seg(a, b):
        return _dot(h, win_ref[:, a:b])

    row = i * tm + lax.broadcasted_iota(jnp.int32, (tm, 1), 0)
    pos = (pos0 + pos_stride * row).astype(F32)
    ang = pos * invf_ref[...]
    cos = jnp.cos(ang)
    sin = jnp.sin(ang) * sgn_ref[...]

    lat = _rms(seg(_O_CKV, _O_KA), gkv_ref[...])
    lat_ref[0] = lat
    kr = seg(_O_KA, _O_KB) * cos + seg(_O_KB, _O_GM) * sin
    kpe_ref[0] = kr[:, :MLA_ROPE_DIM]
    kcat_ref[0, :, :KV_LORA_RANK] = lat.astype(BF16)
    kcat_ref[0, :, KV_LORA_RANK:] = kr.astype(BF16)

    gmla_ref[0] = _silu(seg(_O_GM, _O_SQ)).astype(BF16)
    sbq_ref[0] = (seg(_O_SQ, _O_SK) * SB_SCALE).astype(BF16)
    k = seg(_O_SK, _O_SV)
    sbk_ref[0] = k
    sbkb_ref[0] = k.astype(BF16)
    v = seg(_O_SV, _O_GS)
    sbv_ref[0] = v
    sbvb_ref[0] = v.astype(BF16)
    gsb_ref[0] = _silu(seg(_O_GS, _W_IN_EXT)).astype(BF16)

    cqn = _rms(seg(_O_CQ, _O_CKV), gq_ref[...]).astype(BF16)
    for hd in range(MLA_HEADS):
        a = hd * LANE
        qn = _dot(cqn, wq_ref[:, a:a + LANE]).astype(BF16)
        qa = _dot(qn, wuk_ref[hd]) * MLA_SCALE
        ra = MLA_HEADS * LANE + a
        rb = 2 * MLA_HEADS * LANE + a
        qp = (_dot(cqn, wq_ref[:, ra:ra + LANE]) * cos
              + _dot(cqn, wq_ref[:, rb:rb + LANE]) * sin) * MLA_SCALE
        qcat_ref[0, hd, :, :KV_LORA_RANK] = qa.astype(BF16)
        qcat_ref[0, hd, :, KV_LORA_RANK:] = qp.astype(BF16)


def _proj_in(x, gpre, win, gq, gkv, wq, wuk, invf, sgn, *, tm, pos0, pos_stride):
    nb, s, _ = x.shape
    grid = (nb, s // tm)

    def rows(width):
        return pl.BlockSpec((1, tm, width), lambda b, i: (b, i, 0))

    def out(width, dt):
        return jax.ShapeDtypeStruct((nb, s, width), dt)

    return pl.pallas_call(
        functools.partial(_proj_in_kernel, tm=tm, pos0=pos0, pos_stride=pos_stride),
        grid=grid,
        in_specs=[rows(D_MODEL), _const_spec(gpre.shape), _const_spec(win.shape), _const_spec(gq.shape),
                  _const_spec(gkv.shape), _const_spec(wq.shape), _const_spec(wuk.shape),
                  _const_spec(invf.shape), _const_spec(sgn.shape)],
        out_specs=[rows(KV_LORA_RANK), rows(MLA_ROPE_DIM), rows(QK_DIM), rows(MLA_WIDTH), rows(SB_WIDTH),
                   rows(SB_WIDTH), rows(SB_WIDTH), rows(SB_WIDTH), rows(SB_WIDTH), rows(SB_WIDTH),
                   pl.BlockSpec((1, MLA_HEADS, tm, QK_DIM), lambda b, i: (b, 0, i, 0))],
        out_shape=[out(KV_LORA_RANK, F32), out(MLA_ROPE_DIM, F32), out(QK_DIM, BF16), out(MLA_WIDTH, BF16),
                   out(SB_WIDTH, BF16), out(SB_WIDTH, F32), out(SB_WIDTH, F32), out(SB_WIDTH, BF16),
                   out(SB_WIDTH, BF16), out(SB_WIDTH, BF16),
                   jax.ShapeDtypeStruct((nb, MLA_HEADS, s, QK_DIM), BF16)],
        compiler_params=_params(("parallel", "arbitrary")),
        name="proj_in",
    )(x, gpre, win, gq, gkv, wq, wuk, invf, sgn)


def _mla_prompt_kernel(q_ref, k_ref, g_ref, wuv_ref, o_ref, m_sc, l_sc, acc_sc, *, tq):
    i = pl.program_id(1)
    rows = MLA_HEADS * tq
    q = q_ref[0].reshape(rows, QK_DIM)
    m_sc[...] = jnp.full(m_sc.shape, MASK_VALUE, F32)
    l_sc[...] = jnp.zeros(l_sc.shape, F32)
    acc_sc[...] = jnp.zeros(acc_sc.shape, F32)

    def step(start, masked):
        kc = k_ref[0, pl.ds(start, tq), :]
        s = _dot_nt(q, kc)
        if masked:
            qpos = lax.broadcasted_iota(jnp.int32, (tq, tq), 0)
            kpos = lax.broadcasted_iota(jnp.int32, (tq, tq), 1)
            s = jnp.where((kpos <= qpos)[None], s.reshape(MLA_HEADS, tq, tq), MASK_VALUE).reshape(rows, tq)
        m_prev = m_sc[...]
        m_new = jnp.maximum(m_prev, jnp.max(s, axis=-1, keepdims=True))
        alpha = jnp.exp(m_prev - m_new)
        p = jnp.exp(s - m_new)
        l_sc[...] = alpha * l_sc[...] + jnp.sum(p, axis=-1, keepdims=True)
        acc_sc[...] = alpha * acc_sc[...] + _dot(p.astype(BF16), kc[:, :KV_LORA_RANK])
        m_sc[...] = m_new

    def body(j, carry):
        step(pl.multiple_of(j * tq, tq), False)
        return carry

    lax.fori_loop(0, i, body, 0)
    step(pl.multiple_of(i * tq, tq), True)

    o = (acc_sc[...] / l_sc[...]).astype(BF16)
    for hd in range(MLA_HEADS):
        oh = _dot(o[hd * tq:(hd + 1) * tq], wuv_ref[hd])
        a = hd * MLA_V_DIM
        o_ref[0, :, a:a + MLA_V_DIM] = (oh * g_ref[0, :, a:a + MLA_V_DIM].astype(F32)).astype(BF16)


def _mla_prompt(qcat, kcat, gmla, wuv, *, tq):
    nb, _, s, _ = qcat.shape
    rows = MLA_HEADS * tq
    return pl.pallas_call(
        functools.partial(_mla_prompt_kernel, tq=tq),
        grid=(nb, s // tq),
        in_specs=[pl.BlockSpec((1, MLA_HEADS, tq, QK_DIM), lambda b, i: (b, 0, i, 0)),
                  pl.BlockSpec((1, s, QK_DIM), lambda b, i: (b, 0, 0)),
                  pl.BlockSpec((1, tq, MLA_WIDTH), lambda b, i: (b, i, 0)),
                  _const_spec(wuv.shape)],
        out_specs=pl.BlockSpec((1, tq, MLA_WIDTH), lambda b, i: (b, i, 0)),
        out_shape=jax.ShapeDtypeStruct((nb, s, MLA_WIDTH), BF16),
        scratch_shapes=[pltpu.VMEM((rows, 1), F32), pltpu.VMEM((rows, 1), F32),
                        pltpu.VMEM((rows, KV_LORA_RANK), F32)],
        compiler_params=_params(("parallel", "arbitrary")),
        name="mla_prompt",
    )(qcat, kcat, gmla, wuv)


def _sb_weights(z, lsm, tail):
    return jnp.exp(z + lsm + tail)


def _neg_softplus(z):
    return -(jnp.maximum(z, 0.0) + jnp.log1p(jnp.exp(-jnp.abs(z))))


def _split_bf16(x):
    hi = x.astype(BF16)
    lo = (x - hi.astype(F32)).astype(BF16)
    return hi, lo


def _sb_prompt_kernel(q_ref, k_ref, v_ref, g_ref, u_ref, o_ref, acc_sc, carry_sc, *, tq, tk):
    i = pl.program_id(2)
    q = q_ref[0]
    acc_sc[...] = jnp.zeros(acc_sc.shape, F32)
    carry_sc[...] = jnp.zeros(carry_sc.shape, F32)
    nkb = (i + 1) * (tq // tk)
    qpos = i * tq + lax.broadcasted_iota(jnp.int32, (tq, tk), 0)
    kofs = lax.broadcasted_iota(jnp.int32, (tq, tk), 1)

    def body(jj, carry):
        start = pl.multiple_of((nkb - 1 - jj) * tk, tk)
        k = k_ref[0, pl.ds(start, tk), :]
        v = v_ref[0, pl.ds(start, tk), :]
        z = _dot_nt(q, k)
        mask = (start + kofs) < qpos
        lsm = jnp.where(mask, _neg_softplus(z), 0.0)
        hi, lo = _split_bf16(lsm)
        tail = _dot(jnp.concatenate([hi, lo], axis=1), u_ref[...]) + carry_sc[...]
        w = jnp.where(mask, _sb_weights(z, lsm, tail), 0.0)
        acc_sc[...] += _dot(w.astype(BF16), v)
        carry_sc[...] += jnp.sum(lsm, axis=-1, keepdims=True)
        return carry

    lax.fori_loop(0, nkb, body, 0)
    o_ref[0] = (acc_sc[...] * g_ref[0].astype(F32)).astype(BF16)


def _sb_prompt(sbq, sbkb, sbvb, gsb, u2, *, tq, tk):
    nb, s, _ = sbq.shape
    d = SB_HEAD_DIM

    def qspec():
        return pl.BlockSpec((1, tq, d), lambda b, h, i: (b, i, h))

    def kspec():
        return pl.BlockSpec((1, s, d), lambda b, h, i: (b, 0, h))

    return pl.pallas_call(
        functools.partial(_sb_prompt_kernel, tq=tq, tk=tk),
        grid=(nb, SB_HEADS, s // tq),
        in_specs=[qspec(), kspec(), kspec(), qspec(), _const_spec(u2.shape)],
        out_specs=qspec(),
        out_shape=jax.ShapeDtypeStruct((nb, s, SB_WIDTH), BF16),
        scratch_shapes=[pltpu.VMEM((tq, d), F32), pltpu.VMEM((tq, 1), F32)],
        compiler_params=_params(("parallel", "parallel", "arbitrary")),
        name="sb_prompt",
    )(sbq, sbkb, sbvb, gsb, u2)


def _mla_decode_kernel(pt_ref, q_ref, self_ref, *refs, pages):
    lat_refs = refs[:pages]
    kpe_refs = refs[pages:2 * pages]
    o_ref, m_sc, l_sc, acc_sc = refs[2 * pages:]
    c = pl.program_id(1)

    @pl.when(c == 0)
    def _():
        m_sc[...] = jnp.full(m_sc.shape, MASK_VALUE, F32)
        l_sc[...] = jnp.zeros(l_sc.shape, F32)
        acc_sc[...] = jnp.zeros(acc_sc.shape, F32)

    q = q_ref[0]
    lat = jnp.concatenate([r[0] for r in lat_refs], axis=0).astype(BF16)
    kpe = jnp.concatenate([r[0] for r in kpe_refs], axis=0).astype(BF16)
    s = (_dot_nt(q[:, :KV_LORA_RANK], lat)
         + _dot_nt(q[:, KV_LORA_RANK:KV_LORA_RANK + MLA_ROPE_DIM], kpe))
    m_prev = m_sc[...]
    m_new = jnp.maximum(m_prev, jnp.max(s, axis=-1, keepdims=True))
    alpha = jnp.exp(m_prev - m_new)
    p = jnp.exp(s - m_new)
    l_sc[...] = alpha * l_sc[...] + jnp.sum(p, axis=-1, keepdims=True)
    acc_sc[...] = alpha * acc_sc[...] + _dot(p.astype(BF16), lat)
    m_sc[...] = m_new

    @pl.when(c == pl.num_programs(1) - 1)
    def _():
        ks = self_ref[0].astype(F32)
        s_self = jnp.sum(q.astype(F32) * ks, axis=-1, keepdims=True)
        m_old = m_sc[...]
        m_fin = jnp.maximum(m_old, s_self)
        a = jnp.exp(m_old - m_fin)
        p_self = jnp.exp(s_self - m_fin)
        l_fin = a * l_sc[...] + p_self
        acc = a * acc_sc[...] + p_self * ks[:, :KV_LORA_RANK]
        o_ref[0] = acc / l_fin


def _mla_decode(page_table_flat, q, kself, cache_lat, cache_kpe, *, pages):
    nb = q.shape[0]
    n_pages = page_table_flat.shape[0] // nb
    page = cache_lat.shape[1]

    def page_spec(width, p):
        return pl.BlockSpec((1, page, width),
                            lambda b, c, pt, p=p: (pt[b * n_pages + c * pages + p], 0, 0))

    grid_spec = pltpu.PrefetchScalarGridSpec(
        num_scalar_prefetch=1,
        grid=(nb, n_pages // pages),
        in_specs=([pl.BlockSpec((1, HEADS_PAD, QK_DIM), lambda b, c, pt: (b, 0, 0)),
                   pl.BlockSpec((1, 1, QK_DIM), lambda b, c, pt: (b, 0, 0))]
                  + [page_spec(KV_LORA_RANK, p) for p in range(pages)]
                  + [page_spec(MLA_ROPE_DIM, p) for p in range(pages)]),
        out_specs=pl.BlockSpec((1, HEADS_PAD, KV_LORA_RANK), lambda b, c, pt: (b, 0, 0)),
        scratch_shapes=[pltpu.VMEM((HEADS_PAD, 1), F32), pltpu.VMEM((HEADS_PAD, 1), F32),
                        pltpu.VMEM((HEADS_PAD, KV_LORA_RANK), F32)],
    )
    return pl.pallas_call(
        functools.partial(_mla_decode_kernel, pages=pages),
        grid_spec=grid_spec,
        out_shape=jax.ShapeDtypeStruct((nb, HEADS_PAD, KV_LORA_RANK), F32),
        compiler_params=_params(("parallel", "arbitrary")),
        name="mla_decode",
    )(page_table_flat, q, kself, *([cache_lat] * pages), *([cache_kpe] * pages))


def _mla_up_kernel(o_ref, wuv_ref, g_ref, mix_ref):
    for hd in range(MLA_HEADS):
        a = hd * MLA_V_DIM
        oh = _dot(o_ref[hd], wuv_ref[hd])
        mix_ref[:, a:a + MLA_V_DIM] = (oh * g_ref[:, a:a + MLA_V_DIM].astype(F32)).astype(BF16)


def _mla_up(olat_t, wuv, gmla):
    nb = gmla.shape[0]
    return pl.pallas_call(
        _mla_up_kernel,
        out_shape=jax.ShapeDtypeStruct((nb, MLA_WIDTH), BF16),
        compiler_params=pltpu.CompilerParams(vmem_limit_bytes=VMEM_LIMIT),
        name="mla_up",
    )(olat_t, wuv, gmla)


def _sb_decode_kernel(pt_ref, qbd_ref, g_ref, u_ref, e_ref, *refs, pages):
    k_refs = refs[:pages]
    v_refs = refs[pages:2 * pages]
    o_ref, acc_sc, carry_sc = refs[2 * pages:]
    c = pl.program_id(1)

    @pl.when(c == 0)
    def _():
        acc_sc[...] = jnp.zeros(acc_sc.shape, F32)
        carry_sc[...] = jnp.zeros(carry_sc.shape, F32)

    k = jnp.concatenate([r[0] for r in k_refs], axis=0).astype(BF16)
    v = jnp.concatenate([r[0] for r in v_refs], axis=0)
    z = _dot(k, qbd_ref[0])
    lsm = _neg_softplus(z)
    hi, lo = _split_bf16(lsm)
    tail_hi = _dot(u_ref[...], hi)
    tail_lo = _dot(u_ref[...], lo)
    tail = tail_hi + tail_lo + carry_sc[...]
    w = _sb_weights(z, lsm, tail)
    wexp = _dot(w.astype(BF16), e_ref[...])
    keys = v.shape[0]
    acc_sc[...] += jnp.sum((wexp * v).reshape(keys // 8, 8, SB_WIDTH), axis=0)
    carry_sc[...] += jnp.sum(lsm, axis=0, keepdims=True)

    @pl.when(c == pl.num_programs(1) - 1)
    def _():
        o = jnp.sum(acc_sc[...], axis=0, keepdims=True)
        o_ref[0] = (o * g_ref[0].astype(F32)).astype(BF16)


def _sb_decode(page_table_flat, qbd, gsb, u, e, cache_k, cache_v, *, pages):
    nb = qbd.shape[0]
    n_pages = page_table_flat.shape[0] // nb
    page = cache_k.shape[1]
    steps = n_pages // pages

    def page_spec(p):
        return pl.BlockSpec((1, page, SB_WIDTH),
                            lambda b, c, pt, p=p: (pt[b * n_pages + (steps - 1 - c) * pages + p], 0, 0))

    grid_spec = pltpu.PrefetchScalarGridSpec(
        num_scalar_prefetch=1,
        grid=(nb, steps),
        in_specs=([pl.BlockSpec((1, SB_WIDTH, LANE), lambda b, c, pt: (b, 0, 0)),
                   pl.BlockSpec((1, 1, SB_WIDTH), lambda b, c, pt: (b, 0, 0)),
                   pl.BlockSpec(u.shape, lambda b, c, pt: (0, 0)),
                   pl.BlockSpec(e.shape, lambda b, c, pt: (0, 0))]
                  + [page_spec(p) for p in range(pages)] * 2),
        out_specs=pl.BlockSpec((1, 1, SB_WIDTH), lambda b, c, pt: (b, 0, 0)),
        scratch_shapes=[pltpu.VMEM((8, SB_WIDTH), F32), pltpu.VMEM((1, LANE), F32)],
    )
    return pl.pallas_call(
        functools.partial(_sb_decode_kernel, pages=pages),
        grid_spec=grid_spec,
        out_shape=jax.ShapeDtypeStruct((nb, 1, SB_WIDTH), BF16),
        compiler_params=_params(("parallel", "arbitrary")),
        name="sb_decode",
    )(page_table_flat, qbd, gsb, u, e, *([cache_k] * pages), *([cache_v] * pages))


def _proj_out_kernel(mm_ref, ms_ref, wa_ref, wb_ref, g_ref, x_ref, y_ref):
    y = _dot(mm_ref[...], wa_ref[...]) + _dot(ms_ref[...], wb_ref[...])
    y_ref[...] = x_ref[...] + _rms(y, g_ref[...])


def _proj_out(mix_mla, mix_sb, wa, wb, gpost, x, *, tm):
    n = x.shape[0]

    def rows(width):
        return pl.BlockSpec((tm, width), lambda i: (i, 0))

    return pl.pallas_call(
        _proj_out_kernel,
        grid=(n // tm,),
        in_specs=[rows(MLA_WIDTH), rows(SB_WIDTH), _const_spec(wa.shape), _const_spec(wb.shape),
                  _const_spec(gpost.shape), rows(D_MODEL)],
        out_specs=rows(D_MODEL),
        out_shape=jax.ShapeDtypeStruct((n, D_MODEL), F32),
        compiler_params=_params(("parallel",)),
        name="proj_out",
    )(mix_mla, mix_sb, wa, wb, gpost, x)


def _prep_weights(w_in, w_q_b, w_kv_b, w_out):
    half = MLA_ROPE_DIM // 2
    swap = jnp.concatenate([jnp.arange(half, MLA_ROPE_DIM), jnp.arange(half)])
    o = [0]
    for width in (Q_LORA_RANK, KV_LORA_RANK, MLA_ROPE_DIM, MLA_WIDTH, SB_WIDTH, SB_WIDTH, SB_WIDTH, SB_WIDTH):
        o.append(o[-1] + width)
    cq, ckv, kpe, gm, sq, sk, sv, gs = (w_in[:, o[j]:o[j + 1]] for j in range(8))
    zpad = jnp.zeros((D_MODEL, ROPE_PAD - MLA_ROPE_DIM), w_in.dtype)
    win = jnp.concatenate([cq, ckv, kpe, zpad, kpe[:, swap], zpad, gm, sq, sk, sv, gs], axis=1).astype(BF16)

    wq = w_q_b.reshape(Q_LORA_RANK, MLA_HEADS, MLA_NOPE_DIM + MLA_ROPE_DIM)
    nope = wq[..., :MLA_NOPE_DIM].reshape(Q_LORA_RANK, MLA_HEADS * MLA_NOPE_DIM)
    rope = wq[..., MLA_NOPE_DIM:]
    rpad = jnp.zeros((Q_LORA_RANK, MLA_HEADS, ROPE_PAD - MLA_ROPE_DIM), w_q_b.dtype)
    rope_a = jnp.concatenate([rope, rpad], axis=-1).reshape(Q_LORA_RANK, MLA_HEADS * ROPE_PAD)
    rope_b = jnp.concatenate([rope[..., swap], rpad], axis=-1).reshape(Q_LORA_RANK, MLA_HEADS * ROPE_PAD)
    wq_ext = jnp.concatenate([nope, rope_a, rope_b], axis=1).astype(BF16)

    wuk = jnp.transpose(w_kv_b[..., :MLA_NOPE_DIM], (1, 2, 0)).astype(BF16)
    wuv = jnp.transpose(w_kv_b[..., MLA_NOPE_DIM:], (1, 0, 2)).astype(BF16)
    wa = w_out[:MLA_WIDTH].astype(BF16)
    wb = w_out[MLA_WIDTH:].astype(BF16)
    return win, wq_ext, wuk, wuv, wa, wb


def _rope_consts():
    half = MLA_ROPE_DIM // 2
    inv_freq = ROPE_BASE ** (-jnp.arange(half, dtype=F32) / half)
    invf = jnp.tile(inv_freq, ROPE_PAD // half)[None]
    sgn = jnp.tile(jnp.concatenate([-jnp.ones(half, F32), jnp.ones(half, F32)]), ROPE_PAD // MLA_ROPE_DIM)[None]
    return invf, sgn


def _suffix_matrix(n):
    j = lax.broadcasted_iota(jnp.int32, (n, n), 0)
    s = lax.broadcasted_iota(jnp.int32, (n, n), 1)
    return (j > s).astype(BF16)


def kernel(x_prompt, x_sample, cache_mla_latent, cache_mla_krope, cache_sb_k, cache_sb_v, page_table,
           w_in, g_q_a, w_q_b, g_kv_a, w_kv_b, w_out, g_pre, g_post):
    assert w_in.shape[0] == 1, "single layer"
    nb, seq, _ = x_prompt.shape
    dec_b, dec_s, _ = x_sample.shape
    assert dec_s == 1
    n_pool, page = cache_mla_latent.shape[1], cache_mla_latent.shape[2]
    n_pages = page_table.shape[1]
    past_len = n_pages * page

    win, wq_ext, wuk, wuv, wa, wb = _prep_weights(w_in[0], w_q_b[0], w_kv_b[0], w_out[0])
    invf, sgn = _rope_consts()
    gpre, gq, gkv, gpost = g_pre[0][None], g_q_a[0][None], g_kv_a[0][None], g_post[0][None]

    proj = functools.partial(_proj_in, gpre=gpre, win=win, gq=gq, gkv=gkv, wq=wq_ext, wuk=wuk,
                             invf=invf, sgn=sgn)

    (p_lat, p_kpe, p_kcat, p_gmla, p_sbq, p_sbk, p_sbv, p_sbkb, p_sbvb, p_gsb, p_qcat) = proj(
        x_prompt, tm=256, pos0=0, pos_stride=1)
    mix_mla = _mla_prompt(p_qcat, p_kcat, p_gmla, wuv, tq=256)
    tk = 128
    u_row = jnp.concatenate([_suffix_matrix(tk)] * 2, axis=0)
    mix_sb = _sb_prompt(p_sbq, p_sbkb, p_sbvb, p_gsb, u_row, tq=256, tk=tk)
    y_prompt = _proj_out(mix_mla.reshape(nb * seq, MLA_WIDTH), mix_sb.reshape(nb * seq, SB_WIDTH),
                         wa, wb, gpost, x_prompt.reshape(nb * seq, D_MODEL), tm=512).reshape(x_prompt.shape)

    xs = x_sample.reshape(1, dec_b, D_MODEL)
    (s_lat, s_kpe, s_kcat, s_gmla, s_sbq, s_sbk, s_sbv, _, _, s_gsb, s_qcat) = proj(
        xs, tm=dec_b, pos0=past_len, pos_stride=0)
    pt_flat = page_table.reshape(-1)
    q_dec = jnp.pad(jnp.transpose(s_qcat[0], (1, 0, 2)), ((0, 0), (0, HEADS_PAD - MLA_HEADS), (0, 0)))
    olat = _mla_decode(pt_flat, q_dec, s_kcat.reshape(dec_b, 1, QK_DIM),
                       cache_mla_latent[0], cache_mla_krope[0], pages=16)
    olat_t = jnp.transpose(olat[:, :MLA_HEADS], (1, 0, 2)).astype(BF16)
    mix_mla_s = _mla_up(olat_t, wuv, s_gmla[0])

    sb_pages = 4
    head_of_row = jnp.arange(SB_WIDTH) // SB_HEAD_DIM
    qbd = jnp.where(head_of_row[None, :, None] == jnp.arange(LANE)[None, None, :],
                    s_sbq[0][:, :, None], jnp.zeros((), BF16))
    keys = sb_pages * page
    jj = lax.broadcasted_iota(jnp.int32, (keys, keys), 1)
    kk = lax.broadcasted_iota(jnp.int32, (keys, keys), 0)
    u_col = (jj > kk).astype(BF16)
    expand = (jnp.arange(LANE)[:, None] == head_of_row[None, :]).astype(BF16)
    mix_sb_s = _sb_decode(pt_flat, qbd, s_gsb.reshape(dec_b, 1, SB_WIDTH), u_col, expand,
                          cache_sb_k[0].reshape(n_pool, page, SB_WIDTH),
                          cache_sb_v[0].reshape(n_pool, page, SB_WIDTH), pages=sb_pages)
    y_sample = _proj_out(mix_mla_s, mix_sb_s.reshape(dec_b, SB_WIDTH), wa, wb, gpost,
                         x_sample.reshape(dec_b, D_MODEL), tm=dec_b).reshape(x_sample.shape)

    def heads(a):
        return a.reshape(*a.shape[:-1], SB_HEADS, SB_HEAD_DIM)

    return (y_prompt, y_sample,
            p_lat[None], p_kpe[None], heads(p_sbk)[None], heads(p_sbv)[None],
            s_lat.reshape(1, dec_b, 1, KV_LORA_RANK), s_kpe.reshape(1, dec_b, 1, MLA_ROPE_DIM),
            heads(s_sbk).reshape(1, dec_b, 1, SB_HEADS, SB_HEAD_DIM),
            heads(s_sbv).reshape(1, dec_b, 1, SB_HEADS, SB_HEAD_DIM))
```

```python
import functools
import math

import jax
import jax.numpy as jnp
from jax import lax
from jax.experimental import pallas as pl
from jax.experimental.pallas import tpu as pltpu

F32 = jnp.float32
BF16 = jnp.bfloat16

D_MODEL = 2048
MLA_HEADS = 12
MLA_NOPE_DIM = 128
MLA_ROPE_DIM = 64
MLA_V_DIM = 128
Q_LORA_RANK = 512
KV_LORA_RANK = 256
MLA_WIDTH = MLA_HEADS * MLA_V_DIM
SB_HEADS = 4
SB_HEAD_DIM = 128
SB_WIDTH = SB_HEADS * SB_HEAD_DIM
ROPE_BASE = 10000.0
EPS = 1e-6
MASK_VALUE = -1e30

LANE = 128
ROPE_PAD = LANE
QK_DIM = KV_LORA_RANK + ROPE_PAD
HEADS_PAD = 16
VMEM_LIMIT = 56 * 1024 * 1024

_O_CQ = 0
_O_CKV = _O_CQ + Q_LORA_RANK
_O_KA = _O_CKV + KV_LORA_RANK
_O_KB = _O_KA + ROPE_PAD
_O_GM = _O_KB + ROPE_PAD
_O_SQ = _O_GM + MLA_WIDTH
_O_SK = _O_SQ + SB_WIDTH
_O_SV = _O_SK + SB_WIDTH
_O_GS = _O_SV + SB_WIDTH
_W_IN_EXT = _O_GS + SB_WIDTH

SB_EXIT = -104.0

MLA_SCALE = 1.0 / math.sqrt(MLA_NOPE_DIM + MLA_ROPE_DIM)
SB_SCALE = 1.0 / math.sqrt(SB_HEAD_DIM)


def _params(sem):
    return pltpu.CompilerParams(dimension_semantics=sem, vmem_limit_bytes=VMEM_LIMIT)


def _const_spec(shape):
    nd = len(shape)
    return pl.BlockSpec(shape, lambda *_: (0,) * nd, pipeline_mode=pl.Buffered(1))


def _rms(x, g):
    return x * lax.rsqrt(jnp.mean(x * x, axis=-1, keepdims=True) + EPS) * g


def _silu(x):
    return x / (1.0 + jnp.exp(-x))


def _dot(a, b):
    return jnp.dot(a, b, preferred_element_type=F32)


def _dot_nt(a, b):
    return lax.dot_general(a, b, (((1,), (1,)), ((), ())), preferred_element_type=F32)


def _proj_in_kernel(x_ref, gpre_ref, win_ref, gq_ref, gkv_ref, wq_ref, wuk_ref, invf_ref, sgn_ref,
                    lat_ref, kpe_ref, kcat_ref, gmla_ref, sbq_ref, sbk_ref, sbv_ref,
                    sbkb_ref, sbvb_ref, gsb_ref, qcat_ref, *, tm, pos0, pos_stride):
    i = pl.program_id(1)
    h = _rms(x_ref[0], gpre_ref[...]).astype(BF16)

    def seg(a, b):
        return _dot(h, win_ref[:, a:b])

    row = i * tm + lax.broadcasted_iota(jnp.int32, (tm, 1), 0)
    pos = (pos0 + pos_stride * row).astype(F32)
    ang = pos * invf_ref[...]
    cos = jnp.cos(ang)
    sin = jnp.sin(ang) * sgn_ref[...]

    lat = _rms(seg(_O_CKV, _O_KA), gkv_ref[...])
    lat_ref[0] = lat
    kr = seg(_O_KA, _O_KB) * cos + seg(_O_KB, _O_GM) * sin
    kpe_ref[0] = kr[:, :MLA_ROPE_DIM]
    kcat_ref[0, :, :KV_LORA_RANK] = lat.astype(BF16)
    kcat_ref[0, :, KV_LORA_RANK:] = kr.astype(BF16)

    gmla_ref[0] = _silu(seg(_O_GM, _O_SQ)).astype(BF16)
    sbq_ref[0] = (seg(_O_SQ, _O_SK) * SB_SCALE).astype(BF16)
    k = seg(_O_SK, _O_SV)
    sbk_ref[0] = k
    sbkb_ref[0] = k.astype(BF16)
    v = seg(_O_SV, _O_GS)
    sbv_ref[0] = v
    sbvb_ref[0] = v.astype(BF16)
    gsb_ref[0] = _silu(seg(_O_GS, _W_IN_EXT)).astype(BF16)

    cqn = _rms(seg(_O_CQ, _O_CKV), gq_ref[...]).astype(BF16)
    for hd in range(MLA_HEADS):
        a = hd * LANE
        qn = _dot(cqn, wq_ref[:, a:a + LANE]).astype(BF16)
        qa = _dot(qn, wuk_ref[hd]) * MLA_SCALE
        ra = MLA_HEADS * LANE + a
        rb = 2 * MLA_HEADS * LANE + a
        qp = (_dot(cqn, wq_ref[:, ra:ra + LANE]) * cos
              + _dot(cqn, wq_ref[:, rb:rb + LANE]) * sin) * MLA_SCALE
        qcat_ref[0, hd, :, :KV_LORA_RANK] = qa.astype(BF16)
        qcat_ref[0, hd, :, KV_LORA_RANK:] = qp.astype(BF16)


def _proj_in(x, gpre, win, gq, gkv, wq, wuk, invf, sgn, *, tm, pos0, pos_stride):
    nb, s, _ = x.shape
    grid = (nb, s // tm)

    def rows(width):
        return pl.BlockSpec((1, tm, width), lambda b, i: (b, i, 0))

    def out(width, dt):
        return jax.ShapeDtypeStruct((nb, s, width), dt)

    return pl.pallas_call(
        functools.partial(_proj_in_kernel, tm=tm, pos0=pos0, pos_stride=pos_stride),
        grid=grid,
        in_specs=[rows(D_MODEL), _const_spec(gpre.shape), _const_spec(win.shape), _const_spec(gq.shape),
                  _const_spec(gkv.shape), _const_spec(wq.shape), _const_spec(wuk.shape),
                  _const_spec(invf.shape), _const_spec(sgn.shape)],
        out_specs=[rows(KV_LORA_RANK), rows(MLA_ROPE_DIM), rows(QK_DIM), rows(MLA_WIDTH), rows(SB_WIDTH),
                   rows(SB_WIDTH), rows(SB_WIDTH), rows(SB_WIDTH), rows(SB_WIDTH), rows(SB_WIDTH),
                   pl.BlockSpec((1, MLA_HEADS, tm, QK_DIM), lambda b, i: (b, 0, i, 0))],
        out_shape=[out(KV_LORA_RANK, F32), out(MLA_ROPE_DIM, F32), out(QK_DIM, BF16), out(MLA_WIDTH, BF16),
                   out(SB_WIDTH, BF16), out(SB_WIDTH, F32), out(SB_WIDTH, F32), out(SB_WIDTH, BF16),
                   out(SB_WIDTH, BF16), out(SB_WIDTH, BF16),
                   jax.ShapeDtypeStruct((nb, MLA_HEADS, s, QK_DIM), BF16)],
        compiler_params=_params(("parallel", "arbitrary")),
        name="proj_in",
    )(x, gpre, win, gq, gkv, wq, wuk, invf, sgn)


def _mla_prompt_kernel(q_ref, k_ref, g_ref, wuv_ref, o_ref, m_sc, l_sc, acc_sc, *, tq):
    i = pl.program_id(1)
    rows = MLA_HEADS * tq
    wide = 2 * tq
    q = q_ref[0].reshape(rows, QK_DIM)
    m_sc[...] = jnp.full(m_sc.shape, MASK_VALUE, F32)
    l_sc[...] = jnp.zeros(l_sc.shape, F32)
    acc_sc[...] = jnp.zeros(acc_sc.shape, F32)

    def lanes(x, width):
        return jnp.concatenate([x] * (width // LANE), axis=1)

    def step(start, width, q_off):
        kc = k_ref[0, pl.ds(start, width), :]
        s = _dot_nt(q, kc)
        if q_off is not None:
            qpos = q_off + lax.broadcasted_iota(jnp.int32, (tq, width), 0)
            kpos = lax.broadcasted_iota(jnp.int32, (tq, width), 1)
            s = jnp.where((kpos <= qpos)[None], s.reshape(MLA_HEADS, tq, width), MASK_VALUE).reshape(rows, width)
        m_prev = m_sc[...]
        m_new = jnp.maximum(m_prev, jnp.max(s, axis=-1, keepdims=True))
        alpha = jnp.exp(m_prev - m_new)
        p = jnp.exp(s - lanes(m_new, width))
        l_sc[...] = alpha * l_sc[...] + jnp.sum(p, axis=-1, keepdims=True)
        acc_sc[...] = lanes(alpha, KV_LORA_RANK) * acc_sc[...] + _dot(p.astype(BF16), kc[:, :KV_LORA_RANK])
        m_sc[...] = m_new

    def body(j, carry):
        step(pl.multiple_of(j * wide, wide), wide, None)
        return carry

    lax.fori_loop(0, i // 2, body, 0)

    @pl.when(i % 2 == 0)
    def _():
        step(pl.multiple_of(i * tq, tq), tq, 0)

    @pl.when(i % 2 == 1)
    def _():
        step(pl.multiple_of((i - 1) * tq, wide), wide, tq)

    o = (acc_sc[...] * lanes(1.0 / l_sc[...], KV_LORA_RANK)).astype(BF16)
    for hd in range(MLA_HEADS):
        oh = _dot(o[hd * tq:(hd + 1) * tq], wuv_ref[hd])
        a = hd * MLA_V_DIM
        o_ref[0, :, a:a + MLA_V_DIM] = (oh * g_ref[0, :, a:a + MLA_V_DIM].astype(F32)).astype(BF16)


def _mla_prompt(qcat, kcat, gmla, wuv, *, tq):
    nb, _, s, _ = qcat.shape
    rows = MLA_HEADS * tq
    return pl.pallas_call(
        functools.partial(_mla_prompt_kernel, tq=tq),
        grid=(nb, s // tq),
        in_specs=[pl.BlockSpec((1, MLA_HEADS, tq, QK_DIM), lambda b, i: (b, 0, i, 0)),
                  pl.BlockSpec((1, s, QK_DIM), lambda b, i: (b, 0, 0)),
                  pl.BlockSpec((1, tq, MLA_WIDTH), lambda b, i: (b, i, 0)),
                  _const_spec(wuv.shape)],
        out_specs=pl.BlockSpec((1, tq, MLA_WIDTH), lambda b, i: (b, i, 0)),
        out_shape=jax.ShapeDtypeStruct((nb, s, MLA_WIDTH), BF16),
        scratch_shapes=[pltpu.VMEM((rows, LANE), F32), pltpu.VMEM((rows, LANE), F32),
                        pltpu.VMEM((rows, KV_LORA_RANK), F32)],
        compiler_params=_params(("parallel", "arbitrary")),
        name="mla_prompt",
    )(qcat, kcat, gmla, wuv)


def _sb_weights(z, lsm, tail):
    return jnp.exp(z + lsm + tail)


def _neg_softplus(z):
    return -(jnp.maximum(z, 0.0) + jnp.log1p(jnp.exp(-jnp.abs(z))))


def _split_bf16(x):
    hi = x.astype(BF16)
    lo = (x - hi.astype(F32)).astype(BF16)
    return hi, lo


def _sb_prompt_kernel(q_ref, k_ref, v_ref, g_ref, u_ref, o_ref, acc_sc, carry_sc, *, tq, tk):
    i = pl.program_id(2)
    q = q_ref[0]
    acc_sc[...] = jnp.zeros(acc_sc.shape, F32)
    carry_sc[...] = jnp.zeros(carry_sc.shape, F32)
    nkb = (i + 1) * (tq // tk)
    qpos = i * tq + lax.broadcasted_iota(jnp.int32, (tq, tk), 0)
    kofs = lax.broadcasted_iota(jnp.int32, (tq, tk), 1)

    def body(state):
        jj, _ = state
        start = pl.multiple_of((nkb - 1 - jj) * tk, tk)
        k = k_ref[0, pl.ds(start, tk), :]
        v = v_ref[0, pl.ds(start, tk), :]
        z = _dot_nt(q, k)
        mask = (start + kofs) < qpos
        lsm = jnp.where(mask, _neg_softplus(z), 0.0)
        hi, lo = _split_bf16(lsm)
        tail = _dot(jnp.concatenate([hi, lo], axis=1), u_ref[...]) + carry_sc[...]
        w = jnp.where(mask, _sb_weights(z, lsm, tail), 0.0)
        acc_sc[...] += _dot(w.astype(BF16), v)
        carry = carry_sc[...] + jnp.sum(lsm, axis=-1, keepdims=True)
        carry_sc[...] = carry
        return jj + 1, (jnp.max(carry) < SB_EXIT).astype(jnp.int32)

    lax.while_loop(lambda st: jnp.logical_and(st[0] < nkb, st[1] == 0), body, (jnp.int32(0), jnp.int32(0)))
    o_ref[0] = (acc_sc[...] * g_ref[0].astype(F32)).astype(BF16)


def _sb_prompt(sbq, sbkb, sbvb, gsb, u2, *, tq, tk):
    nb, s, _ = sbq.shape
    d = SB_HEAD_DIM

    def qspec():
        return pl.BlockSpec((1, tq, d), lambda b, h, i: (b, i, h))

    def kspec():
        return pl.BlockSpec((1, s, d), lambda b, h, i: (b, 0, h))

    return pl.pallas_call(
        functools.partial(_sb_prompt_kernel, tq=tq, tk=tk),
        grid=(nb, SB_HEADS, s // tq),
        in_specs=[qspec(), kspec(), kspec(), qspec(), _const_spec(u2.shape)],
        out_specs=qspec(),
        out_shape=jax.ShapeDtypeStruct((nb, s, SB_WIDTH), BF16),
        scratch_shapes=[pltpu.VMEM((tq, d), F32), pltpu.VMEM((tq, 1), F32)],
        compiler_params=_params(("parallel", "parallel", "arbitrary")),
        name="sb_prompt",
    )(sbq, sbkb, sbvb, gsb, u2)


def _mla_decode_kernel(pt_ref, q_ref, self_ref, *refs, pages):
    lat_refs = refs[:pages]
    kpe_refs = refs[pages:2 * pages]
    o_ref, m_sc, l_sc, acc_sc = refs[2 * pages:]
    c = pl.program_id(1)

    @pl.when(c == 0)
    def _():
        m_sc[...] = jnp.full(m_sc.shape, MASK_VALUE, F32)
        l_sc[...] = jnp.zeros(l_sc.shape, F32)
        acc_sc[...] = jnp.zeros(acc_sc.shape, F32)

    q = q_ref[0]
    lat = jnp.concatenate([r[0] for r in lat_refs], axis=0).astype(BF16)
    kpe_t = jnp.concatenate([r[0] for r in kpe_refs], axis=1).astype(BF16)
    s = (_dot_nt(q[:, :KV_LORA_RANK], lat)
         + _dot(q[:, KV_LORA_RANK:KV_LORA_RANK + MLA_ROPE_DIM], kpe_t))
    m_prev = m_sc[...]
    m_new = jnp.maximum(m_prev, jnp.max(s, axis=-1, keepdims=True))
    alpha = jnp.exp(m_prev - m_new)
    p = jnp.exp(s - m_new)
    l_sc[...] = alpha * l_sc[...] + jnp.sum(p, axis=-1, keepdims=True)
    acc_sc[...] = alpha * acc_sc[...] + _dot(p.astype(BF16), lat)
    m_sc[...] = m_new

    @pl.when(c == pl.num_programs(1) - 1)
    def _():
        ks = self_ref[0].astype(F32)
        s_self = jnp.sum(q.astype(F32) * ks, axis=-1, keepdims=True)
        m_old = m_sc[...]
        m_fin = jnp.maximum(m_old, s_self)
        a = jnp.exp(m_old - m_fin)
        p_self = jnp.exp(s_self - m_fin)
        l_fin = a * l_sc[...] + p_self
        acc = a * acc_sc[...] + p_self * ks[:, :KV_LORA_RANK]
        o_ref[0] = acc / l_fin


def _mla_decode(page_table_flat, q, kself, cache_lat, cache_kpe_t, *, pages):
    nb = q.shape[0]
    n_pages = page_table_flat.shape[0] // nb
    page = cache_lat.shape[1]

    def page_spec(shape, p):
        return pl.BlockSpec((1,) + shape,
                            lambda b, c, pt, p=p: (pt[b * n_pages + c * pages + p], 0, 0))

    grid_spec = pltpu.PrefetchScalarGridSpec(
        num_scalar_prefetch=1,
        grid=(nb, n_pages // pages),
        in_specs=([pl.BlockSpec((1, HEADS_PAD, QK_DIM), lambda b, c, pt: (b, 0, 0)),
                   pl.BlockSpec((1, 1, QK_DIM), lambda b, c, pt: (b, 0, 0))]
                  + [page_spec((page, KV_LORA_RANK), p) for p in range(pages)]
                  + [page_spec((MLA_ROPE_DIM, page), p) for p in range(pages)]),
        out_specs=pl.BlockSpec((1, HEADS_PAD, KV_LORA_RANK), lambda b, c, pt: (b, 0, 0)),
        scratch_shapes=[pltpu.VMEM((HEADS_PAD, 1), F32), pltpu.VMEM((HEADS_PAD, 1), F32),
                        pltpu.VMEM((HEADS_PAD, KV_LORA_RANK), F32)],
    )
    return pl.pallas_call(
        functools.partial(_mla_decode_kernel, pages=pages),
        grid_spec=grid_spec,
        out_shape=jax.ShapeDtypeStruct((nb, HEADS_PAD, KV_LORA_RANK), F32),
        compiler_params=_params(("parallel", "arbitrary")),
        name="mla_decode",
    )(page_table_flat, q, kself, *([cache_lat] * pages), *([cache_kpe_t] * pages))


def _mla_up_kernel(o_ref, wuv_ref, g_ref, mix_ref):
    for hd in range(MLA_HEADS):
        a = hd * MLA_V_DIM
        oh = _dot(o_ref[hd], wuv_ref[hd])
        mix_ref[:, a:a + MLA_V_DIM] = (oh * g_ref[:, a:a + MLA_V_DIM].astype(F32)).astype(BF16)


def _mla_up(olat_t, wuv, gmla):
    nb = gmla.shape[0]
    return pl.pallas_call(
        _mla_up_kernel,
        out_shape=jax.ShapeDtypeStruct((nb, MLA_WIDTH), BF16),
        compiler_params=pltpu.CompilerParams(vmem_limit_bytes=VMEM_LIMIT),
        name="mla_up",
    )(olat_t, wuv, gmla)


def _sb_decode_kernel(pt_ref, q4_ref, g_ref, u_ref, ones_ref, k_hbm, v_hbm, o_ref,
                      kbuf, vbuf, sem, acc_sc, carry_sc, *, n_pages):
    b = pl.program_id(0)
    newest = n_pages - 1

    def copies(bb, p):
        slot = p & 1
        idx = pt_ref[bb * n_pages + p]
        return (pltpu.make_async_copy(k_hbm.at[idx], kbuf.at[slot], sem.at[0, slot]),
                pltpu.make_async_copy(v_hbm.at[idx], vbuf.at[slot], sem.at[1, slot]))

    def start(bb, p):
        for cp in copies(bb, p):
            cp.start()

    def wait(bb, p):
        for cp in copies(bb, p):
            cp.wait()

    @pl.when(b == 0)
    def _():
        start(0, newest)

    acc_sc[...] = jnp.zeros(acc_sc.shape, F32)
    carry_sc[...] = jnp.zeros(carry_sc.shape, F32)
    rows = kbuf.shape[1]
    own_lane = ((lax.broadcasted_iota(jnp.int32, (rows, LANE), 0) & (SB_HEADS - 1))
                == lax.broadcasted_iota(jnp.int32, (rows, LANE), 1))
    head_lane = lax.broadcasted_iota(jnp.int32, (1, LANE), 1) < SB_HEADS
    q4 = q4_ref[0]

    def body(state):
        p, _ = state
        slot = p & 1
        wait(b, p)

        @pl.when(p > 0)
        def _():
            start(b, p - 1)

        z = _dot(kbuf[slot].astype(BF16), q4)
        lsm = jnp.where(own_lane, _neg_softplus(z), 0.0)
        hi, lo = _split_bf16(lsm)
        tail = _dot(u_ref[...], hi) + _dot(u_ref[...], lo) + carry_sc[...]
        w = jnp.where(own_lane, _sb_weights(z, lsm, tail), 0.0)
        wrow = _dot(w.astype(BF16), ones_ref[...])
        acc_sc[...] += jnp.sum((wrow * vbuf[slot]).reshape(rows // 8, 8, LANE), axis=0)
        carry = carry_sc[...] + jnp.sum(lsm, axis=0, keepdims=True)
        carry_sc[...] = carry
        live = jnp.max(jnp.where(head_lane, carry, -jnp.inf))
        return p - 1, (live < SB_EXIT).astype(jnp.int32)

    p_end, _ = lax.while_loop(lambda st: jnp.logical_and(st[0] >= 0, st[1] == 0), body,
                              (jnp.int32(newest), jnp.int32(0)))

    @pl.when(p_end >= 0)
    def _():
        wait(b, p_end)

    @pl.when(b + 1 < pl.num_programs(0))
    def _():
        start(b + 1, newest)

    acc = acc_sc[...]
    o = acc[:SB_HEADS] + acc[SB_HEADS:]
    o_ref[0] = (o * g_ref[0].astype(F32)).astype(BF16)


def _sb_decode(page_table_flat, q4, gsb, u, ones, cache_k, cache_v):
    nb = q4.shape[0]
    n_pages = page_table_flat.shape[0] // nb
    rows = cache_k.shape[1]
    grid_spec = pltpu.PrefetchScalarGridSpec(
        num_scalar_prefetch=1,
        grid=(nb,),
        in_specs=[pl.BlockSpec((1, SB_HEAD_DIM, LANE), lambda b, pt: (b, 0, 0)),
                  pl.BlockSpec((1, SB_HEADS, SB_HEAD_DIM), lambda b, pt: (b, 0, 0)),
                  pl.BlockSpec(u.shape, lambda b, pt: (0, 0)),
                  pl.BlockSpec(ones.shape, lambda b, pt: (0, 0)),
                  pl.BlockSpec(memory_space=pl.ANY),
                  pl.BlockSpec(memory_space=pl.ANY)],
        out_specs=pl.BlockSpec((1, SB_HEADS, SB_HEAD_DIM), lambda b, pt: (b, 0, 0)),
        scratch_shapes=[pltpu.VMEM((2, rows, LANE), F32), pltpu.VMEM((2, rows, LANE), F32),
                        pltpu.SemaphoreType.DMA((2, 2)),
                        pltpu.VMEM((8, LANE), F32), pltpu.VMEM((1, LANE), F32)],
    )
    return pl.pallas_call(
        functools.partial(_sb_decode_kernel, n_pages=n_pages),
        grid_spec=grid_spec,
        out_shape=jax.ShapeDtypeStruct((nb, SB_HEADS, SB_HEAD_DIM), BF16),
        compiler_params=_params(("arbitrary",)),
        name="sb_decode",
    )(page_table_flat, q4, gsb, u, ones, cache_k, cache_v)


def _proj_out_kernel(mm_ref, ms_ref, wa_ref, wb_ref, g_ref, x_ref, y_ref):
    y = _dot(mm_ref[...], wa_ref[...]) + _dot(ms_ref[...], wb_ref[...])
    y_ref[...] = x_ref[...] + _rms(y, g_ref[...])


def _proj_out(mix_mla, mix_sb, wa, wb, gpost, x, *, tm):
    n = x.shape[0]

    def rows(width):
        return pl.BlockSpec((tm, width), lambda i: (i, 0))

    return pl.pallas_call(
        _proj_out_kernel,
        grid=(n // tm,),
        in_specs=[rows(MLA_WIDTH), rows(SB_WIDTH), _const_spec(wa.shape), _const_spec(wb.shape),
                  _const_spec(gpost.shape), rows(D_MODEL)],
        out_specs=rows(D_MODEL),
        out_shape=jax.ShapeDtypeStruct((n, D_MODEL), F32),
        compiler_params=_params(("parallel",)),
        name="proj_out",
    )(mix_mla, mix_sb, wa, wb, gpost, x)


def _prep_weights(w_in, w_q_b, w_kv_b, w_out):
    half = MLA_ROPE_DIM // 2
    swap = jnp.concatenate([jnp.arange(half, MLA_ROPE_DIM), jnp.arange(half)])
    o = [0]
    for width in (Q_LORA_RANK, KV_LORA_RANK, MLA_ROPE_DIM, MLA_WIDTH, SB_WIDTH, SB_WIDTH, SB_WIDTH, SB_WIDTH):
        o.append(o[-1] + width)
    cq, ckv, kpe, gm, sq, sk, sv, gs = (w_in[:, o[j]:o[j + 1]] for j in range(8))
    zpad = jnp.zeros((D_MODEL, ROPE_PAD - MLA_ROPE_DIM), w_in.dtype)
    win = jnp.concatenate([cq, ckv, kpe, zpad, kpe[:, swap], zpad, gm, sq, sk, sv, gs], axis=1).astype(BF16)

    wq = w_q_b.reshape(Q_LORA_RANK, MLA_HEADS, MLA_NOPE_DIM + MLA_ROPE_DIM)
    nope = wq[..., :MLA_NOPE_DIM].reshape(Q_LORA_RANK, MLA_HEADS * MLA_NOPE_DIM)
    rope = wq[..., MLA_NOPE_DIM:]
    rpad = jnp.zeros((Q_LORA_RANK, MLA_HEADS, ROPE_PAD - MLA_ROPE_DIM), w_q_b.dtype)
    rope_a = jnp.concatenate([rope, rpad], axis=-1).reshape(Q_LORA_RANK, MLA_HEADS * ROPE_PAD)
    rope_b = jnp.concatenate([rope[..., swap], rpad], axis=-1).reshape(Q_LORA_RANK, MLA_HEADS * ROPE_PAD)
    wq_ext = jnp.concatenate([nope, rope_a, rope_b], axis=1).astype(BF16)

    wuk = jnp.transpose(w_kv_b[..., :MLA_NOPE_DIM], (1, 2, 0)).astype(BF16)
    wuv = jnp.transpose(w_kv_b[..., MLA_NOPE_DIM:], (1, 0, 2)).astype(BF16)
    wa = w_out[:MLA_WIDTH].astype(BF16)
    wb = w_out[MLA_WIDTH:].astype(BF16)
    return win, wq_ext, wuk, wuv, wa, wb


def _rope_consts():
    half = MLA_ROPE_DIM // 2
    inv_freq = ROPE_BASE ** (-jnp.arange(half, dtype=F32) / half)
    invf = jnp.tile(inv_freq, ROPE_PAD // half)[None]
    sgn = jnp.tile(jnp.concatenate([-jnp.ones(half, F32), jnp.ones(half, F32)]), ROPE_PAD // MLA_ROPE_DIM)[None]
    return invf, sgn


def _suffix_matrix(n):
    j = lax.broadcasted_iota(jnp.int32, (n, n), 0)
    s = lax.broadcasted_iota(jnp.int32, (n, n), 1)
    return (j > s).astype(BF16)


def kernel(x_prompt, x_sample, cache_mla_latent, cache_mla_krope, cache_sb_k, cache_sb_v, page_table,
           w_in, g_q_a, w_q_b, g_kv_a, w_kv_b, w_out, g_pre, g_post):
    assert w_in.shape[0] == 1, "single layer"
    nb, seq, _ = x_prompt.shape
    dec_b, dec_s, _ = x_sample.shape
    assert dec_s == 1
    n_pool, page = cache_mla_latent.shape[1], cache_mla_latent.shape[2]
    n_pages = page_table.shape[1]
    past_len = n_pages * page

    win, wq_ext, wuk, wuv, wa, wb = _prep_weights(w_in[0], w_q_b[0], w_kv_b[0], w_out[0])
    invf, sgn = _rope_consts()
    gpre, gq, gkv, gpost = g_pre[0][None], g_q_a[0][None], g_kv_a[0][None], g_post[0][None]

    proj = functools.partial(_proj_in, gpre=gpre, win=win, gq=gq, gkv=gkv, wq=wq_ext, wuk=wuk,
                             invf=invf, sgn=sgn)

    (p_lat, p_kpe, p_kcat, p_gmla, p_sbq, p_sbk, p_sbv, p_sbkb, p_sbvb, p_gsb, p_qcat) = proj(
        x_prompt, tm=256, pos0=0, pos_stride=1)
    mix_mla = _mla_prompt(p_qcat, p_kcat, p_gmla, wuv, tq=256)
    tk = 128
    u_row = jnp.concatenate([_suffix_matrix(tk)] * 2, axis=0)
    mix_sb = _sb_prompt(p_sbq, p_sbkb, p_sbvb, p_gsb, u_row, tq=256, tk=tk)
    y_prompt = _proj_out(mix_mla.reshape(nb * seq, MLA_WIDTH), mix_sb.reshape(nb * seq, SB_WIDTH),
                         wa, wb, gpost, x_prompt.reshape(nb * seq, D_MODEL), tm=512).reshape(x_prompt.shape)

    xs = x_sample.reshape(1, dec_b, D_MODEL)
    (s_lat, s_kpe, s_kcat, s_gmla, s_sbq, s_sbk, s_sbv, _, _, s_gsb, s_qcat) = proj(
        xs, tm=dec_b, pos0=past_len, pos_stride=0)
    pt_flat = page_table.reshape(-1)
    q_dec = jnp.pad(jnp.transpose(s_qcat[0], (1, 0, 2)), ((0, 0), (0, HEADS_PAD - MLA_HEADS), (0, 0)))
    olat = _mla_decode(pt_flat, q_dec, s_kcat.reshape(dec_b, 1, QK_DIM),
                       cache_mla_latent[0], jnp.swapaxes(cache_mla_krope[0], 1, 2), pages=16)
    olat_t = jnp.transpose(olat[:, :MLA_HEADS], (1, 0, 2)).astype(BF16)
    mix_mla_s = _mla_up(olat_t, wuv, s_gmla[0])

    q4 = jnp.pad(jnp.swapaxes(s_sbq[0].reshape(dec_b, SB_HEADS, SB_HEAD_DIM), 1, 2),
                 ((0, 0), (0, 0), (0, LANE - SB_HEADS)))
    rows = page * SB_HEADS
    u_col = _suffix_matrix(rows).T
    mix_sb_s = _sb_decode(pt_flat, q4, s_gsb.reshape(dec_b, SB_HEADS, SB_HEAD_DIM), u_col,
                          jnp.ones((LANE, LANE), BF16),
                          cache_sb_k.reshape(n_pool, rows, SB_HEAD_DIM),
                          cache_sb_v.reshape(n_pool, rows, SB_HEAD_DIM))
    y_sample = _proj_out(mix_mla_s, mix_sb_s.reshape(dec_b, SB_WIDTH), wa, wb, gpost,
                         x_sample.reshape(dec_b, D_MODEL), tm=dec_b).reshape(x_sample.shape)

    def heads(a):
        return a.reshape(*a.shape[:-1], SB_HEADS, SB_HEAD_DIM)

    return (y_prompt, y_sample,
            p_lat[None], p_kpe[None], heads(p_sbk)[None], heads(p_sbv)[None],
            s_lat.reshape(1, dec_b, 1, KV_LORA_RANK), s_kpe.reshape(1, dec_b, 1, MLA_ROPE_DIM),
            heads(s_sbk).reshape(1, dec_b, 1, SB_HEADS, SB_HEAD_DIM),
            heads(s_sbv).reshape(1, dec_b, 1, SB_HEADS, SB_HEAD_DIM))
```

```python
import functools
import math

import jax
import jax.numpy as jnp
from jax import lax
from jax.experimental import pallas as pl
from jax.experimental.pallas import tpu as pltpu

F32 = jnp.float32
BF16 = jnp.bfloat16

D_MODEL = 2048
MLA_HEADS = 12
MLA_NOPE_DIM = 128
MLA_ROPE_DIM = 64
MLA_V_DIM = 128
Q_LORA_RANK = 512
KV_LORA_RANK = 256
MLA_WIDTH = MLA_HEADS * MLA_V_DIM
SB_HEADS = 4
SB_HEAD_DIM = 128
SB_WIDTH = SB_HEADS * SB_HEAD_DIM
ROPE_BASE = 10000.0
EPS = 1e-6
MASK_VALUE = -1e30

LANE = 128
ROPE_PAD = LANE
QK_DIM = KV_LORA_RANK + ROPE_PAD
HEADS_PAD = 16
VMEM_LIMIT = 56 * 1024 * 1024

_O_CQ = 0
_O_CKV = _O_CQ + Q_LORA_RANK
_O_KA = _O_CKV + KV_LORA_RANK
_O_KB = _O_KA + ROPE_PAD
_O_GM = _O_KB + ROPE_PAD
_O_SQ = _O_GM + MLA_WIDTH
_O_SK = _O_SQ + SB_WIDTH
_O_SV = _O_SK + SB_WIDTH
_O_GS = _O_SV + SB_WIDTH
_W_IN_EXT = _O_GS + SB_WIDTH

SB_EXIT = -104.0

MLA_SCALE = 1.0 / math.sqrt(MLA_NOPE_DIM + MLA_ROPE_DIM)
SB_SCALE = 1.0 / math.sqrt(SB_HEAD_DIM)


def _params(sem):
    return pltpu.CompilerParams(dimension_semantics=sem, vmem_limit_bytes=VMEM_LIMIT)


def _const_spec(shape):
    nd = len(shape)
    return pl.BlockSpec(shape, lambda *_: (0,) * nd, pipeline_mode=pl.Buffered(1))


def _rms(x, g):
    return x * lax.rsqrt(jnp.mean(x * x, axis=-1, keepdims=True) + EPS) * g


def _silu(x):
    return x / (1.0 + jnp.exp(-x))


def _dot(a, b):
    return jnp.dot(a, b, preferred_element_type=F32)


def _dot_nt(a, b):
    return lax.dot_general(a, b, (((1,), (1,)), ((), ())), preferred_element_type=F32)


def _proj_in_kernel(x_ref, gpre_ref, win_ref, gq_ref, gkv_ref, wq_ref, wuk_ref, invf_ref, sgn_ref,
                    lat_ref, kpe_ref, kcat_ref, gmla_ref, sbq_ref, sbk_ref, sbv_ref,
                    sbkb_ref, sbvb_ref, gsb_ref, qcat_ref, *, tm, pos0, pos_stride):
    i = pl.program_id(1)
    h = _rms(x_ref[0], gpre_ref[...]).astype(BF16)

    def seg(a, b):
        return _dot(h, win_ref[:, a:b])

    row = i * tm + lax.broadcasted_iota(jnp.int32, (tm, 1), 0)
    pos = (pos0 + pos_stride * row).astype(F32)
    ang = pos * invf_ref[...]
    cos = jnp.cos(ang)
    sin = jnp.sin(ang) * sgn_ref[...]

    lat = _rms(seg(_O_CKV, _O_KA), gkv_ref[...])
    lat_ref[0] = lat
    kr = seg(_O_KA, _O_KB) * cos + seg(_O_KB, _O_GM) * sin
    kpe_ref[0] = kr[:, :MLA_ROPE_DIM]
    kcat_ref[0, :, :KV_LORA_RANK] = lat.astype(BF16)
    kcat_ref[0, :, KV_LORA_RANK:] = kr.astype(BF16)

    gmla_ref[0] = _silu(seg(_O_GM, _O_SQ)).astype(BF16)
    sbq_ref[0] = (seg(_O_SQ, _O_SK) * SB_SCALE).astype(BF16)
    k = seg(_O_SK, _O_SV)
    sbk_ref[0] = k
    sbkb_ref[0] = k.astype(BF16)
    v = seg(_O_SV, _O_GS)
    sbv_ref[0] = v
    sbvb_ref[0] = v.astype(BF16)
    gsb_ref[0] = _silu(seg(_O_GS, _W_IN_EXT)).astype(BF16)

    cqn = _rms(seg(_O_CQ, _O_CKV), gq_ref[...]).astype(BF16)
    for hd in range(MLA_HEADS):
        a = hd * LANE
        qn = _dot(cqn, wq_ref[:, a:a + LANE]).astype(BF16)
        qa = _dot(qn, wuk_ref[hd]) * MLA_SCALE
        ra = MLA_HEADS * LANE + a
        rb = 2 * MLA_HEADS * LANE + a
        qp = (_dot(cqn, wq_ref[:, ra:ra + LANE]) * cos
              + _dot(cqn, wq_ref[:, rb:rb + LANE]) * sin) * MLA_SCALE
        qcat_ref[0, hd, :, :KV_LORA_RANK] = qa.astype(BF16)
        qcat_ref[0, hd, :, KV_LORA_RANK:] = qp.astype(BF16)


def _proj_in(x, gpre, win, gq, gkv, wq, wuk, invf, sgn, *, tm, pos0, pos_stride):
    nb, s, _ = x.shape
    grid = (nb, s // tm)

    def rows(width):
        return pl.BlockSpec((1, tm, width), lambda b, i: (b, i, 0))

    def out(width, dt):
        return jax.ShapeDtypeStruct((nb, s, width), dt)

    return pl.pallas_call(
        functools.partial(_proj_in_kernel, tm=tm, pos0=pos0, pos_stride=pos_stride),
        grid=grid,
        in_specs=[rows(D_MODEL), _const_spec(gpre.shape), _const_spec(win.shape), _const_spec(gq.shape),
                  _const_spec(gkv.shape), _const_spec(wq.shape), _const_spec(wuk.shape),
                  _const_spec(invf.shape), _const_spec(sgn.shape)],
        out_specs=[rows(KV_LORA_RANK), rows(MLA_ROPE_DIM), rows(QK_DIM), rows(MLA_WIDTH), rows(SB_WIDTH),
                   rows(SB_WIDTH), rows(SB_WIDTH), rows(SB_WIDTH), rows(SB_WIDTH), rows(SB_WIDTH),
                   pl.BlockSpec((1, MLA_HEADS, tm, QK_DIM), lambda b, i: (b, 0, i, 0))],
        out_shape=[out(KV_LORA_RANK, F32), out(MLA_ROPE_DIM, F32), out(QK_DIM, BF16), out(MLA_WIDTH, BF16),
                   out(SB_WIDTH, BF16), out(SB_WIDTH, F32), out(SB_WIDTH, F32), out(SB_WIDTH, BF16),
                   out(SB_WIDTH, BF16), out(SB_WIDTH, BF16),
                   jax.ShapeDtypeStruct((nb, MLA_HEADS, s, QK_DIM), BF16)],
        compiler_params=_params(("parallel", "arbitrary")),
        name="proj_in",
    )(x, gpre, win, gq, gkv, wq, wuk, invf, sgn)


def _mla_prompt_kernel(q_ref, k_ref, g_ref, wuv_ref, o_ref, m_sc, l_sc, acc_sc, *, tq):
    i = pl.program_id(1)
    rows = MLA_HEADS * tq
    wide = 2 * tq
    q = q_ref[0].reshape(rows, QK_DIM)
    m_sc[...] = jnp.full(m_sc.shape, MASK_VALUE, F32)
    l_sc[...] = jnp.zeros(l_sc.shape, F32)
    acc_sc[...] = jnp.zeros(acc_sc.shape, F32)

    def lanes(x, width):
        return jnp.concatenate([x] * (width // LANE), axis=1)

    def step(start, width, q_off):
        kc = k_ref[0, pl.ds(start, width), :]
        s = _dot_nt(q, kc)
        if q_off is not None:
            qpos = q_off + lax.broadcasted_iota(jnp.int32, (tq, width), 0)
            kpos = lax.broadcasted_iota(jnp.int32, (tq, width), 1)
            s = jnp.where((kpos <= qpos)[None], s.reshape(MLA_HEADS, tq, width), MASK_VALUE).reshape(rows, width)
        m_prev = m_sc[...]
        m_new = jnp.maximum(m_prev, jnp.max(s, axis=-1, keepdims=True))
        alpha = jnp.exp(m_prev - m_new)
        p = jnp.exp(s - lanes(m_new, width))
        l_sc[...] = alpha * l_sc[...] + jnp.sum(p, axis=-1, keepdims=True)
        acc_sc[...] = lanes(alpha, KV_LORA_RANK) * acc_sc[...] + _dot(p.astype(BF16), kc[:, :KV_LORA_RANK])
        m_sc[...] = m_new

    def body(j, carry):
        step(pl.multiple_of(j * wide, wide), wide, None)
        return carry

    lax.fori_loop(0, i // 2, body, 0)

    @pl.when(i % 2 == 0)
    def _():
        step(pl.multiple_of(i * tq, tq), tq, 0)

    @pl.when(i % 2 == 1)
    def _():
        step(pl.multiple_of((i - 1) * tq, wide), wide, tq)

    o = (acc_sc[...] * lanes(1.0 / l_sc[...], KV_LORA_RANK)).astype(BF16)
    for hd in range(MLA_HEADS):
        oh = _dot(o[hd * tq:(hd + 1) * tq], wuv_ref[hd])
        a = hd * MLA_V_DIM
        o_ref[0, :, a:a + MLA_V_DIM] = (oh * g_ref[0, :, a:a + MLA_V_DIM].astype(F32)).astype(BF16)


def _mla_prompt(qcat, kcat, gmla, wuv, *, tq):
    nb, _, s, _ = qcat.shape
    rows = MLA_HEADS * tq
    return pl.pallas_call(
        functools.partial(_mla_prompt_kernel, tq=tq),
        grid=(nb, s // tq),
        in_specs=[pl.BlockSpec((1, MLA_HEADS, tq, QK_DIM), lambda b, i: (b, 0, i, 0)),
                  pl.BlockSpec((1, s, QK_DIM), lambda b, i: (b, 0, 0)),
                  pl.BlockSpec((1, tq, MLA_WIDTH), lambda b, i: (b, i, 0)),
                  _const_spec(wuv.shape)],
        out_specs=pl.BlockSpec((1, tq, MLA_WIDTH), lambda b, i: (b, i, 0)),
        out_shape=jax.ShapeDtypeStruct((nb, s, MLA_WIDTH), BF16),
        scratch_shapes=[pltpu.VMEM((rows, LANE), F32), pltpu.VMEM((rows, LANE), F32),
                        pltpu.VMEM((rows, KV_LORA_RANK), F32)],
        compiler_params=_params(("parallel", "arbitrary")),
        name="mla_prompt",
    )(qcat, kcat, gmla, wuv)


def _sb_weights(z, lsm, tail):
    return jnp.exp(z + lsm + tail)


def _neg_softplus(z):
    return -(jnp.maximum(z, 0.0) + jnp.log1p(jnp.exp(-jnp.abs(z))))


def _split_bf16(x):
    hi = x.astype(BF16)
    lo = (x - hi.astype(F32)).astype(BF16)
    return hi, lo


def _sb_prompt_kernel(q_ref, k_ref, v_ref, g_ref, u_ref, o_ref, acc_sc, carry_sc, *, tq):
    i = pl.program_id(1)
    acc_sc[...] = jnp.zeros(acc_sc.shape, F32)
    carry_sc[...] = jnp.zeros(carry_sc.shape, F32)
    below_diag = (lax.broadcasted_iota(jnp.int32, (tq, tq), 1)
                  < lax.broadcasted_iota(jnp.int32, (tq, tq), 0))

    def block(kb, diagonal):
        start = pl.multiple_of(kb * tq, tq)
        live = None
        for hd in range(SB_HEADS):
            cols = slice(hd * SB_HEAD_DIM, (hd + 1) * SB_HEAD_DIM)
            k = k_ref[0, pl.ds(start, tq), cols]
            v = v_ref[0, pl.ds(start, tq), cols]
            z = _dot_nt(q_ref[0, :, cols], k)
            lsm = _neg_softplus(z)
            if diagonal:
                lsm = jnp.where(below_diag, lsm, 0.0)
            hi, lo = _split_bf16(lsm)
            tail = _dot(jnp.concatenate([hi, lo], axis=1), u_ref[...]) + carry_sc[hd]
            w = _sb_weights(z, lsm, tail)
            if diagonal:
                w = jnp.where(below_diag, w, 0.0)
            acc_sc[hd] += _dot(w.astype(BF16), v)
            carry = carry_sc[hd] + jnp.sum(lsm, axis=-1, keepdims=True)
            carry_sc[hd] = carry
            top = jnp.max(carry)
            live = top if live is None else jnp.maximum(live, top)
        return live

    def body(state):
        kb, _ = state
        return kb - 1, (block(kb, False) < SB_EXIT).astype(jnp.int32)

    done = (block(i, True) < SB_EXIT).astype(jnp.int32)
    lax.while_loop(lambda st: jnp.logical_and(st[0] >= 0, st[1] == 0), body, (i - 1, done))
    for hd in range(SB_HEADS):
        cols = slice(hd * SB_HEAD_DIM, (hd + 1) * SB_HEAD_DIM)
        o_ref[0, :, cols] = (acc_sc[hd] * g_ref[0, :, cols].astype(F32)).astype(BF16)


def _sb_prompt(sbq, sbkb, sbvb, gsb, u2, *, tq):
    nb, s, _ = sbq.shape

    def qspec():
        return pl.BlockSpec((1, tq, SB_WIDTH), lambda b, i: (b, i, 0))

    def kspec():
        return pl.BlockSpec((1, s, SB_WIDTH), lambda b, i: (b, 0, 0))

    return pl.pallas_call(
        functools.partial(_sb_prompt_kernel, tq=tq),
        grid=(nb, s // tq),
        in_specs=[qspec(), kspec(), kspec(), qspec(), _const_spec(u2.shape)],
        out_specs=qspec(),
        out_shape=jax.ShapeDtypeStruct((nb, s, SB_WIDTH), BF16),
        scratch_shapes=[pltpu.VMEM((SB_HEADS, tq, SB_HEAD_DIM), F32), pltpu.VMEM((SB_HEADS, tq, 1), F32)],
        compiler_params=_params(("parallel", "arbitrary")),
        name="sb_prompt",
    )(sbq, sbkb, sbvb, gsb, u2)


def _mla_decode_kernel(pt_ref, q_ref, self_ref, lat_hbm, kpe_hbm, o_ref, lat_buf, kpe_buf, sem, *, n_pages):
    b = pl.program_id(0)
    slot = b & 1
    page = lat_hbm.shape[1]

    def copies(bb, sl, p):
        idx = pt_ref[bb * n_pages + p]
        off = pl.multiple_of(p * page, page)
        return (pltpu.make_async_copy(lat_hbm.at[idx], lat_buf.at[sl, pl.ds(off, page), :], sem.at[0, sl]),
                pltpu.make_async_copy(kpe_hbm.at[idx], kpe_buf.at[sl, :, pl.ds(off, page)], sem.at[1, sl]))

    def start_all(bb, sl):
        def issue(p, carry):
            for cp in copies(bb, sl, p):
                cp.start()
            return carry
        lax.fori_loop(0, n_pages, issue, 0)

    def wait_all(bb, sl):
        def drain(p, carry):
            for cp in copies(bb, sl, p):
                cp.wait()
            return carry
        lax.fori_loop(0, n_pages, drain, 0)

    @pl.when(b == 0)
    def _():
        start_all(0, 0)

    @pl.when(b + 1 < pl.num_programs(0))
    def _():
        start_all(b + 1, 1 - slot)

    wait_all(b, slot)
    q = q_ref[0]
    lat = lat_buf[slot].astype(BF16)
    kpe_t = kpe_buf[slot].astype(BF16)
    s = (_dot_nt(q[:, :KV_LORA_RANK], lat)
         + _dot(q[:, KV_LORA_RANK:KV_LORA_RANK + MLA_ROPE_DIM], kpe_t))
    ks = self_ref[0].astype(F32)
    s_self = jnp.sum(q.astype(F32) * ks, axis=-1, keepdims=True)
    m = jnp.maximum(jnp.max(s, axis=-1, keepdims=True), s_self)
    p = jnp.exp(s - m)
    p_self = jnp.exp(s_self - m)
    l = jnp.sum(p, axis=-1, keepdims=True) + p_self
    acc = _dot(p.astype(BF16), lat) + p_self * ks[:, :KV_LORA_RANK]
    o_ref[0] = acc / l


def _mla_decode(page_table_flat, q, kself, cache_lat, cache_kpe_t):
    nb = q.shape[0]
    n_pages = page_table_flat.shape[0] // nb
    past = n_pages * cache_lat.shape[1]
    grid_spec = pltpu.PrefetchScalarGridSpec(
        num_scalar_prefetch=1,
        grid=(nb,),
        in_specs=[pl.BlockSpec((1, HEADS_PAD, QK_DIM), lambda b, pt: (b, 0, 0)),
                  pl.BlockSpec((1, 1, QK_DIM), lambda b, pt: (b, 0, 0)),
                  pl.BlockSpec(memory_space=pl.ANY),
                  pl.BlockSpec(memory_space=pl.ANY)],
        out_specs=pl.BlockSpec((1, HEADS_PAD, KV_LORA_RANK), lambda b, pt: (b, 0, 0)),
        scratch_shapes=[pltpu.VMEM((2, past, KV_LORA_RANK), F32), pltpu.VMEM((2, MLA_ROPE_DIM, past), F32),
                        pltpu.SemaphoreType.DMA((2, 2))],
    )
    return pl.pallas_call(
        functools.partial(_mla_decode_kernel, n_pages=n_pages),
        grid_spec=grid_spec,
        out_shape=jax.ShapeDtypeStruct((nb, HEADS_PAD, KV_LORA_RANK), F32),
        compiler_params=_params(("arbitrary",)),
        name="mla_decode",
    )(page_table_flat, q, kself, cache_lat, cache_kpe_t)


def _mla_up_kernel(o_ref, wuv_ref, g_ref, mix_ref):
    for hd in range(MLA_HEADS):
        a = hd * MLA_V_DIM
        oh = _dot(o_ref[hd], wuv_ref[hd])
        mix_ref[:, a:a + MLA_V_DIM] = (oh * g_ref[:, a:a + MLA_V_DIM].astype(F32)).astype(BF16)


def _mla_up(olat_t, wuv, gmla):
    nb = gmla.shape[0]
    return pl.pallas_call(
        _mla_up_kernel,
        out_shape=jax.ShapeDtypeStruct((nb, MLA_WIDTH), BF16),
        compiler_params=pltpu.CompilerParams(vmem_limit_bytes=VMEM_LIMIT),
        name="mla_up",
    )(olat_t, wuv, gmla)


def _sb_decode_kernel(pt_ref, q4_ref, g_ref, u_ref, ones_ref, k_hbm, v_hbm, o_ref,
                      kbuf, vbuf, sem, acc_sc, carry_sc, *, n_pages):
    b = pl.program_id(0)
    n_pairs = n_pages // 2

    def copies(bb, t):
        slot = t & 1
        out = []
        for j in range(2):
            idx = pt_ref[bb * n_pages + (n_pages - 1 - j) - 2 * t]
            out.append(pltpu.make_async_copy(k_hbm.at[idx], kbuf.at[slot, j], sem.at[0, slot, j]))
            out.append(pltpu.make_async_copy(v_hbm.at[idx], vbuf.at[slot, j], sem.at[1, slot, j]))
        return out

    def start(bb, t):
        for cp in copies(bb, t):
            cp.start()

    def wait(bb, t):
        for cp in copies(bb, t):
            cp.wait()

    @pl.when(b == 0)
    def _():
        start(0, 0)

    acc_sc[...] = jnp.zeros(acc_sc.shape, F32)
    carry_sc[...] = jnp.zeros(carry_sc.shape, F32)
    rows = kbuf.shape[2]
    own_lane = ((lax.broadcasted_iota(jnp.int32, (rows, LANE), 0) & (SB_HEADS - 1))
                == lax.broadcasted_iota(jnp.int32, (rows, LANE), 1))
    head_lane = lax.broadcasted_iota(jnp.int32, (1, LANE), 1) < SB_HEADS
    q4 = q4_ref[0]

    def body(state):
        t, _ = state
        slot = t & 1
        wait(b, t)

        @pl.when(t + 1 < n_pairs)
        def _():
            start(b, t + 1)

        carry = carry_sc[...]
        contrib = None
        for j in range(2):
            z = _dot(kbuf[slot, j].astype(BF16), q4)
            lsm = jnp.where(own_lane, _neg_softplus(z), 0.0)
            hi, lo = _split_bf16(lsm)
            tail = _dot(u_ref[...], hi) + _dot(u_ref[...], lo) + carry
            w = jnp.where(own_lane, _sb_weights(z, lsm, tail), 0.0)
            wrow = _dot(w.astype(BF16), ones_ref[...])
            part = jnp.sum((wrow * vbuf[slot, j]).reshape(rows // 8, 8, LANE), axis=0)
            contrib = part if contrib is None else contrib + part
            carry = carry + jnp.sum(lsm, axis=0, keepdims=True)
        acc_sc[...] += contrib
        carry_sc[...] = carry
        live = jnp.max(jnp.where(head_lane, carry, -jnp.inf))
        return t + 1, (live < SB_EXIT).astype(jnp.int32)

    t_end, _ = lax.while_loop(lambda st: jnp.logical_and(st[0] < n_pairs, st[1] == 0), body,
                              (jnp.int32(0), jnp.int32(0)))

    @pl.when(t_end < n_pairs)
    def _():
        wait(b, t_end)

    @pl.when(b + 1 < pl.num_programs(0))
    def _():
        start(b + 1, 0)

    acc = acc_sc[...]
    o = acc[:SB_HEADS] + acc[SB_HEADS:]
    o_ref[0] = (o * g_ref[0].astype(F32)).astype(BF16)


def _sb_decode(page_table_flat, q4, gsb, u, ones, cache_k, cache_v):
    nb = q4.shape[0]
    n_pages = page_table_flat.shape[0] // nb
    rows = cache_k.shape[1]
    grid_spec = pltpu.PrefetchScalarGridSpec(
        num_scalar_prefetch=1,
        grid=(nb,),
        in_specs=[pl.BlockSpec((1, SB_HEAD_DIM, LANE), lambda b, pt: (b, 0, 0)),
                  pl.BlockSpec((1, SB_HEADS, SB_HEAD_DIM), lambda b, pt: (b, 0, 0)),
                  pl.BlockSpec(u.shape, lambda b, pt: (0, 0)),
                  pl.BlockSpec(ones.shape, lambda b, pt: (0, 0)),
                  pl.BlockSpec(memory_space=pl.ANY),
                  pl.BlockSpec(memory_space=pl.ANY)],
        out_specs=pl.BlockSpec((1, SB_HEADS, SB_HEAD_DIM), lambda b, pt: (b, 0, 0)),
        scratch_shapes=[pltpu.VMEM((2, 2, rows, LANE), F32), pltpu.VMEM((2, 2, rows, LANE), F32),
                        pltpu.SemaphoreType.DMA((2, 2, 2)),
                        pltpu.VMEM((8, LANE), F32), pltpu.VMEM((1, LANE), F32)],
    )
    return pl.pallas_call(
        functools.partial(_sb_decode_kernel, n_pages=n_pages),
        grid_spec=grid_spec,
        out_shape=jax.ShapeDtypeStruct((nb, SB_HEADS, SB_HEAD_DIM), BF16),
        compiler_params=_params(("arbitrary",)),
        name="sb_decode",
    )(page_table_flat, q4, gsb, u, ones, cache_k, cache_v)


def _proj_out_kernel(mm_ref, ms_ref, wa_ref, wb_ref, g_ref, x_ref, y_ref):
    y = _dot(mm_ref[...], wa_ref[...]) + _dot(ms_ref[...], wb_ref[...])
    y_ref[...] = x_ref[...] + _rms(y, g_ref[...])


def _proj_out(mix_mla, mix_sb, wa, wb, gpost, x, *, tm):
    n = x.shape[0]

    def rows(width):
        return pl.BlockSpec((tm, width), lambda i: (i, 0))

    return pl.pallas_call(
        _proj_out_kernel,
        grid=(n // tm,),
        in_specs=[rows(MLA_WIDTH), rows(SB_WIDTH), _const_spec(wa.shape), _const_spec(wb.shape),
                  _const_spec(gpost.shape), rows(D_MODEL)],
        out_specs=rows(D_MODEL),
        out_shape=jax.ShapeDtypeStruct((n, D_MODEL), F32),
        compiler_params=_params(("parallel",)),
        name="proj_out",
    )(mix_mla, mix_sb, wa, wb, gpost, x)


def _prep_weights(w_in, w_q_b, w_kv_b, w_out):
    half = MLA_ROPE_DIM // 2
    swap = jnp.concatenate([jnp.arange(half, MLA_ROPE_DIM), jnp.arange(half)])
    o = [0]
    for width in (Q_LORA_RANK, KV_LORA_RANK, MLA_ROPE_DIM, MLA_WIDTH, SB_WIDTH, SB_WIDTH, SB_WIDTH, SB_WIDTH):
        o.append(o[-1] + width)
    cq, ckv, kpe, gm, sq, sk, sv, gs = (w_in[:, o[j]:o[j + 1]] for j in range(8))
    zpad = jnp.zeros((D_MODEL, ROPE_PAD - MLA_ROPE_DIM), w_in.dtype)
    win = jnp.concatenate([cq, ckv, kpe, zpad, kpe[:, swap], zpad, gm, sq, sk, sv, gs], axis=1).astype(BF16)

    wq = w_q_b.reshape(Q_LORA_RANK, MLA_HEADS, MLA_NOPE_DIM + MLA_ROPE_DIM)
    nope = wq[..., :MLA_NOPE_DIM].reshape(Q_LORA_RANK, MLA_HEADS * MLA_NOPE_DIM)
    rope = wq[..., MLA_NOPE_DIM:]
    rpad = jnp.zeros((Q_LORA_RANK, MLA_HEADS, ROPE_PAD - MLA_ROPE_DIM), w_q_b.dtype)
    rope_a = jnp.concatenate([rope, rpad], axis=-1).reshape(Q_LORA_RANK, MLA_HEADS * ROPE_PAD)
    rope_b = jnp.concatenate([rope[..., swap], rpad], axis=-1).reshape(Q_LORA_RANK, MLA_HEADS * ROPE_PAD)
    wq_ext = jnp.concatenate([nope, rope_a, rope_b], axis=1).astype(BF16)

    wuk = jnp.transpose(w_kv_b[..., :MLA_NOPE_DIM], (1, 2, 0)).astype(BF16)
    wuv = jnp.transpose(w_kv_b[..., MLA_NOPE_DIM:], (1, 0, 2)).astype(BF16)
    wa = w_out[:MLA_WIDTH].astype(BF16)
    wb = w_out[MLA_WIDTH:].astype(BF16)
    return win, wq_ext, wuk, wuv, wa, wb


def _rope_consts():
    half = MLA_ROPE_DIM // 2
    inv_freq = ROPE_BASE ** (-jnp.arange(half, dtype=F32) / half)
    invf = jnp.tile(inv_freq, ROPE_PAD // half)[None]
    sgn = jnp.tile(jnp.concatenate([-jnp.ones(half, F32), jnp.ones(half, F32)]), ROPE_PAD // MLA_ROPE_DIM)[None]
    return invf, sgn


def _suffix_matrix(n):
    j = lax.broadcasted_iota(jnp.int32, (n, n), 0)
    s = lax.broadcasted_iota(jnp.int32, (n, n), 1)
    return (j > s).astype(BF16)


def kernel(x_prompt, x_sample, cache_mla_latent, cache_mla_krope, cache_sb_k, cache_sb_v, page_table,
           w_in, g_q_a, w_q_b, g_kv_a, w_kv_b, w_out, g_pre, g_post):
    assert w_in.shape[0] == 1, "single layer"
    nb, seq, _ = x_prompt.shape
    dec_b, dec_s, _ = x_sample.shape
    assert dec_s == 1
    n_pool, page = cache_mla_latent.shape[1], cache_mla_latent.shape[2]
    n_pages = page_table.shape[1]
    past_len = n_pages * page

    win, wq_ext, wuk, wuv, wa, wb = _prep_weights(w_in[0], w_q_b[0], w_kv_b[0], w_out[0])
    invf, sgn = _rope_consts()
    gpre, gq, gkv, gpost = g_pre[0][None], g_q_a[0][None], g_kv_a[0][None], g_post[0][None]

    proj = functools.partial(_proj_in, gpre=gpre, win=win, gq=gq, gkv=gkv, wq=wq_ext, wuk=wuk,
                             invf=invf, sgn=sgn)

    (p_lat, p_kpe, p_kcat, p_gmla, p_sbq, p_sbk, p_sbv, p_sbkb, p_sbvb, p_gsb, p_qcat) = proj(
        x_prompt, tm=256, pos0=0, pos_stride=1)
    mix_mla = _mla_prompt(p_qcat, p_kcat, p_gmla, wuv, tq=256)
    sb_tq = 256
    u_row = jnp.concatenate([_suffix_matrix(sb_tq)] * 2, axis=0)
    mix_sb = _sb_prompt(p_sbq, p_sbkb, p_sbvb, p_gsb, u_row, tq=sb_tq)
    y_prompt = _proj_out(mix_mla.reshape(nb * seq, MLA_WIDTH), mix_sb.reshape(nb * seq, SB_WIDTH),
                         wa, wb, gpost, x_prompt.reshape(nb * seq, D_MODEL), tm=512).reshape(x_prompt.shape)

    xs = x_sample.reshape(1, dec_b, D_MODEL)
    (s_lat, s_kpe, s_kcat, s_gmla, s_sbq, s_sbk, s_sbv, _, _, s_gsb, s_qcat) = proj(
        xs, tm=dec_b, pos0=past_len, pos_stride=0)
    pt_flat = page_table.reshape(-1)
    q_dec = jnp.pad(jnp.transpose(s_qcat[0], (1, 0, 2)), ((0, 0), (0, HEADS_PAD - MLA_HEADS), (0, 0)))
    olat = _mla_decode(pt_flat, q_dec, s_kcat.reshape(dec_b, 1, QK_DIM),
                       cache_mla_latent[0], jnp.swapaxes(cache_mla_krope[0], 1, 2))
    olat_t = jnp.transpose(olat[:, :MLA_HEADS], (1, 0, 2)).astype(BF16)
    mix_mla_s = _mla_up(olat_t, wuv, s_gmla[0])

    q4 = jnp.pad(jnp.swapaxes(s_sbq[0].reshape(dec_b, SB_HEADS, SB_HEAD_DIM), 1, 2),
                 ((0, 0), (0, 0), (0, LANE - SB_HEADS)))
    rows = page * SB_HEADS
    u_col = _suffix_matrix(rows).T
    mix_sb_s = _sb_decode(pt_flat, q4, s_gsb.reshape(dec_b, SB_HEADS, SB_HEAD_DIM), u_col,
                          jnp.ones((LANE, LANE), BF16),
                          cache_sb_k.reshape(n_pool, rows, SB_HEAD_DIM),
                          cache_sb_v.reshape(n_pool, rows, SB_HEAD_DIM))
    y_sample = _proj_out(mix_mla_s, mix_sb_s.reshape(dec_b, SB_WIDTH), wa, wb, gpost,
                         x_sample.reshape(dec_b, D_MODEL), tm=dec_b).reshape(x_sample.shape)

    def heads(a):
        return a.reshape(*a.shape[:-1], SB_HEADS, SB_HEAD_DIM)

    return (y_prompt, y_sample,
            p_lat[None], p_kpe[None], heads(p_sbk)[None], heads(p_sbv)[None],
            s_lat.reshape(1, dec_b, 1, KV_LORA_RANK), s_kpe.reshape(1, dec_b, 1, MLA_ROPE_DIM),
            heads(s_sbk).reshape(1, dec_b, 1, SB_HEADS, SB_HEAD_DIM),
            heads(s_sbv).reshape(1, dec_b, 1, SB_HEADS, SB_HEAD_DIM))
```

```python
import functools
import math

import jax
import jax.numpy as jnp
from jax import lax
from jax.experimental import pallas as pl
from jax.experimental.pallas import tpu as pltpu

F32 = jnp.float32
BF16 = jnp.bfloat16

D_MODEL = 2048
MLA_HEADS = 12
MLA_NOPE_DIM = 128
MLA_ROPE_DIM = 64
MLA_V_DIM = 128
Q_LORA_RANK = 512
KV_LORA_RANK = 256
MLA_WIDTH = MLA_HEADS * MLA_V_DIM
SB_HEADS = 4
SB_HEAD_DIM = 128
SB_WIDTH = SB_HEADS * SB_HEAD_DIM
ROPE_BASE = 10000.0
EPS = 1e-6
MASK_VALUE = -1e30

LANE = 128
ROPE_PAD = LANE
QK_DIM = KV_LORA_RANK + ROPE_PAD
HEADS_PAD = 16
DECODE_CHUNKS = 1
VMEM_LIMIT = 56 * 1024 * 1024

_HEAD_COLS = Q_LORA_RANK + KV_LORA_RANK
_TAIL_START = _HEAD_COLS + MLA_ROPE_DIM
_T_GM = 0
_T_SQ = _T_GM + MLA_WIDTH
_T_SK = _T_SQ + SB_WIDTH
_T_SV = _T_SK + SB_WIDTH
_T_GS = _T_SV + SB_WIDTH
_TAIL_COLS = _T_GS + SB_WIDTH

SB_EXIT = -104.0

MLA_SCALE = 1.0 / math.sqrt(MLA_NOPE_DIM + MLA_ROPE_DIM)
SB_SCALE = 1.0 / math.sqrt(SB_HEAD_DIM)


def _params(sem):
    return pltpu.CompilerParams(dimension_semantics=sem, vmem_limit_bytes=VMEM_LIMIT)


def _const_spec(shape):
    nd = len(shape)
    return pl.BlockSpec(shape, lambda *_: (0,) * nd, pipeline_mode=pl.Buffered(1))


def _rms(x, g):
    return x * lax.rsqrt(jnp.mean(x * x, axis=-1, keepdims=True) + EPS) * g


def _silu(x):
    return x / (1.0 + jnp.exp(-x))


def _dot(a, b):
    return jnp.dot(a, b, preferred_element_type=F32)


def _dot_nt(a, b):
    return lax.dot_general(a, b, (((1,), (1,)), ((), ())), preferred_element_type=F32)


def _proj_in_kernel(x_ref, gpre_ref, whead_ref, wrope_ref, wtail_ref, gq_ref, gkv_ref, wq_ref, wuk_ref,
                    invf_ref, sgn_ref,
                    lat_ref, kpe_ref, kcat_ref, gmla_ref, sbq_ref, sbk_ref, sbv_ref,
                    sbkb_ref, sbvb_ref, gsb_ref, qcat_ref, *, tm, pos0, pos_stride):
    i = pl.program_id(1)
    h = _rms(x_ref[0], gpre_ref[...]).astype(BF16)

    def seg(w_ref, a, b):
        return _dot(h, w_ref[:, a:b])

    row = i * tm + lax.broadcasted_iota(jnp.int32, (tm, 1), 0)
    pos = (pos0 + pos_stride * row).astype(F32)
    ang = pos * invf_ref[...]
    cos = jnp.cos(ang)
    sin = jnp.sin(ang) * sgn_ref[...]

    lat = _rms(seg(whead_ref, Q_LORA_RANK, _HEAD_COLS), gkv_ref[...])
    lat_ref[0] = lat
    kr = seg(wrope_ref, 0, ROPE_PAD) * cos + seg(wrope_ref, ROPE_PAD, 2 * ROPE_PAD) * sin
    kpe_ref[0] = kr[:, :MLA_ROPE_DIM]
    kcat_ref[0, :, :KV_LORA_RANK] = lat.astype(BF16)
    kcat_ref[0, :, KV_LORA_RANK:] = kr.astype(BF16)

    gmla_ref[0] = _silu(seg(wtail_ref, _T_GM, _T_SQ)).astype(BF16)
    sbq_ref[0] = (seg(wtail_ref, _T_SQ, _T_SK) * SB_SCALE).astype(BF16)
    k = seg(wtail_ref, _T_SK, _T_SV)
    sbk_ref[0] = k
    sbkb_ref[0] = k.astype(BF16)
    v = seg(wtail_ref, _T_SV, _T_GS)
    sbv_ref[0] = v
    sbvb_ref[0] = v.astype(BF16)
    gsb_ref[0] = _silu(seg(wtail_ref, _T_GS, _TAIL_COLS)).astype(BF16)

    cqn = _rms(seg(whead_ref, 0, Q_LORA_RANK), gq_ref[...]).astype(BF16)
    for hd in range(MLA_HEADS):
        a = hd * LANE
        qn = _dot(cqn, wq_ref[:, a:a + LANE]).astype(BF16)
        qa = _dot(qn, wuk_ref[hd]) * MLA_SCALE
        ra = MLA_HEADS * LANE + a
        rb = 2 * MLA_HEADS * LANE + a
        qp = (_dot(cqn, wq_ref[:, ra:ra + LANE]) * cos
              + _dot(cqn, wq_ref[:, rb:rb + LANE]) * sin) * MLA_SCALE
        qcat_ref[0, hd, :, :KV_LORA_RANK] = qa.astype(BF16)
        qcat_ref[0, hd, :, KV_LORA_RANK:] = qp.astype(BF16)


def _proj_in(x, gpre, win, gq, gkv, wq, wuk, invf, sgn, *, tm, pos0, pos_stride):
    whead, wrope, wtail = win
    nb, s, _ = x.shape
    grid = (nb, s // tm)

    def rows(width):
        return pl.BlockSpec((1, tm, width), lambda b, i: (b, i, 0))

    def out(width, dt):
        return jax.ShapeDtypeStruct((nb, s, width), dt)

    return pl.pallas_call(
        functools.partial(_proj_in_kernel, tm=tm, pos0=pos0, pos_stride=pos_stride),
        grid=grid,
        in_specs=[rows(D_MODEL), _const_spec(gpre.shape), _const_spec(whead.shape), _const_spec(wrope.shape),
                  _const_spec(wtail.shape), _const_spec(gq.shape),
                  _const_spec(gkv.shape), _const_spec(wq.shape), _const_spec(wuk.shape),
                  _const_spec(invf.shape), _const_spec(sgn.shape)],
        out_specs=[rows(KV_LORA_RANK), rows(MLA_ROPE_DIM), rows(QK_DIM), rows(MLA_WIDTH), rows(SB_WIDTH),
                   rows(SB_WIDTH), rows(SB_WIDTH), rows(SB_WIDTH), rows(SB_WIDTH), rows(SB_WIDTH),
                   pl.BlockSpec((1, MLA_HEADS, tm, QK_DIM), lambda b, i: (b, 0, i, 0))],
        out_shape=[out(KV_LORA_RANK, F32), out(MLA_ROPE_DIM, F32), out(QK_DIM, BF16), out(MLA_WIDTH, BF16),
                   out(SB_WIDTH, BF16), out(SB_WIDTH, F32), out(SB_WIDTH, F32), out(SB_WIDTH, BF16),
                   out(SB_WIDTH, BF16), out(SB_WIDTH, BF16),
                   jax.ShapeDtypeStruct((nb, MLA_HEADS, s, QK_DIM), BF16)],
        compiler_params=_params(("parallel", "arbitrary")),
        name="proj_in",
    )(x, gpre, whead, wrope, wtail, gq, gkv, wq, wuk, invf, sgn)


def _mla_prompt_kernel(q_ref, k_ref, g_ref, wuv_ref, o_ref, m_sc, l_sc, acc_sc, *, tq):
    i = pl.program_id(1)
    rows = MLA_HEADS * tq
    wide = 2 * tq
    q = q_ref[0].reshape(rows, QK_DIM)
    m_sc[...] = jnp.full(m_sc.shape, MASK_VALUE, F32)
    l_sc[...] = jnp.zeros(l_sc.shape, F32)
    acc_sc[...] = jnp.zeros(acc_sc.shape, F32)

    def lanes(x, width):
        return jnp.concatenate([x] * (width // LANE), axis=1)

    def step(start, width, q_off):
        kc = k_ref[0, pl.ds(start, width), :]
        s = _dot_nt(q, kc)
        if q_off is not None:
            qpos = q_off + lax.broadcasted_iota(jnp.int32, (tq, width), 0)
            kpos = lax.broadcasted_iota(jnp.int32, (tq, width), 1)
            s = jnp.where((kpos <= qpos)[None], s.reshape(MLA_HEADS, tq, width), MASK_VALUE).reshape(rows, width)
        m_prev = m_sc[...]
        m_new = jnp.maximum(m_prev, jnp.max(s, axis=-1, keepdims=True))
        alpha = jnp.exp(m_prev - m_new)
        p = jnp.exp(s - lanes(m_new, width))
        l_sc[...] = alpha * l_sc[...] + jnp.sum(p, axis=-1, keepdims=True)
        acc_sc[...] = lanes(alpha, KV_LORA_RANK) * acc_sc[...] + _dot(p.astype(BF16), kc[:, :KV_LORA_RANK])
        m_sc[...] = m_new

    def body(j, carry):
        step(pl.multiple_of(j * wide, wide), wide, None)
        return carry

    lax.fori_loop(0, i // 2, body, 0)

    @pl.when(i % 2 == 0)
    def _():
        step(pl.multiple_of(i * tq, tq), tq, 0)

    @pl.when(i % 2 == 1)
    def _():
        step(pl.multiple_of((i - 1) * tq, wide), wide, tq)

    o = (acc_sc[...] * lanes(1.0 / l_sc[...], KV_LORA_RANK)).astype(BF16)
    for hd in range(MLA_HEADS):
        oh = _dot(o[hd * tq:(hd + 1) * tq], wuv_ref[hd])
        a = hd * MLA_V_DIM
        o_ref[0, :, a:a + MLA_V_DIM] = (oh * g_ref[0, :, a:a + MLA_V_DIM].astype(F32)).astype(BF16)


def _mla_prompt(qcat, kcat, gmla, wuv, *, tq):
    nb, _, s, _ = qcat.shape
    rows = MLA_HEADS * tq
    return pl.pallas_call(
        functools.partial(_mla_prompt_kernel, tq=tq),
        grid=(nb, s // tq),
        in_specs=[pl.BlockSpec((1, MLA_HEADS, tq, QK_DIM), lambda b, i: (b, 0, i, 0)),
                  pl.BlockSpec((1, s, QK_DIM), lambda b, i: (b, 0, 0)),
                  pl.BlockSpec((1, tq, MLA_WIDTH), lambda b, i: (b, i, 0)),
                  _const_spec(wuv.shape)],
        out_specs=pl.BlockSpec((1, tq, MLA_WIDTH), lambda b, i: (b, i, 0)),
        out_shape=jax.ShapeDtypeStruct((nb, s, MLA_WIDTH), BF16),
        scratch_shapes=[pltpu.VMEM((rows, LANE), F32), pltpu.VMEM((rows, LANE), F32),
                        pltpu.VMEM((rows, KV_LORA_RANK), F32)],
        compiler_params=_params(("parallel", "arbitrary")),
        name="mla_prompt",
    )(qcat, kcat, gmla, wuv)


def _sb_weights(z, lsm, tail):
    return jnp.exp(z + lsm + tail)


def _neg_softplus(z):
    return -(jnp.maximum(z, 0.0) + jnp.log1p(jnp.exp(-jnp.abs(z))))


def _split_bf16(x):
    hi = x.astype(BF16)
    lo = (x - hi.astype(F32)).astype(BF16)
    return hi, lo


def _sb_prompt_kernel(q_ref, k_ref, v_ref, g_ref, u_ref, o_ref, acc_sc, carry_sc, *, tq):
    i = pl.program_id(1)
    acc_sc[...] = jnp.zeros(acc_sc.shape, F32)
    carry_sc[...] = jnp.zeros(carry_sc.shape, F32)
    below_diag = (lax.broadcasted_iota(jnp.int32, (tq, tq), 1)
                  < lax.broadcasted_iota(jnp.int32, (tq, tq), 0))

    def block(kb, diagonal):
        start = pl.multiple_of(kb * tq, tq)
        live = None
        for hd in range(SB_HEADS):
            cols = slice(hd * SB_HEAD_DIM, (hd + 1) * SB_HEAD_DIM)
            k = k_ref[0, pl.ds(start, tq), cols]
            v = v_ref[0, pl.ds(start, tq), cols]
            z = _dot_nt(q_ref[0, :, cols], k)
            lsm = _neg_softplus(z)
            if diagonal:
                lsm = jnp.where(below_diag, lsm, 0.0)
            hi, lo = _split_bf16(lsm)
            tail = _dot(jnp.concatenate([hi, lo], axis=1), u_ref[...]) + carry_sc[hd]
            w = _sb_weights(z, lsm, tail)
            if diagonal:
                w = jnp.where(below_diag, w, 0.0)
            acc_sc[hd] += _dot(w.astype(BF16), v)
            carry = carry_sc[hd] + jnp.sum(lsm, axis=-1, keepdims=True)
            carry_sc[hd] = carry
            top = jnp.max(carry)
            live = top if live is None else jnp.maximum(live, top)
        return live

    def body(state):
        kb, _ = state
        return kb - 1, (block(kb, False) < SB_EXIT).astype(jnp.int32)

    done = (block(i, True) < SB_EXIT).astype(jnp.int32)
    lax.while_loop(lambda st: jnp.logical_and(st[0] >= 0, st[1] == 0), body, (i - 1, done))
    for hd in range(SB_HEADS):
        cols = slice(hd * SB_HEAD_DIM, (hd + 1) * SB_HEAD_DIM)
        o_ref[0, :, cols] = (acc_sc[hd] * g_ref[0, :, cols].astype(F32)).astype(BF16)


def _sb_prompt(sbq, sbkb, sbvb, gsb, u2, *, tq):
    nb, s, _ = sbq.shape

    def qspec():
        return pl.BlockSpec((1, tq, SB_WIDTH), lambda b, i: (b, i, 0))

    def kspec():
        return pl.BlockSpec((1, s, SB_WIDTH), lambda b, i: (b, 0, 0))

    return pl.pallas_call(
        functools.partial(_sb_prompt_kernel, tq=tq),
        grid=(nb, s // tq),
        in_specs=[qspec(), kspec(), kspec(), qspec(), _const_spec(u2.shape)],
        out_specs=qspec(),
        out_shape=jax.ShapeDtypeStruct((nb, s, SB_WIDTH), BF16),
        scratch_shapes=[pltpu.VMEM((SB_HEADS, tq, SB_HEAD_DIM), F32), pltpu.VMEM((SB_HEADS, tq, 1), F32)],
        compiler_params=_params(("parallel", "arbitrary")),
        name="sb_prompt",
    )(sbq, sbkb, sbvb, gsb, u2)


def _mla_decode_kernel(pt_ref, q_ref, self_ref, lat_hbm, kpe_hbm, o_ref, lat_buf, kpe_buf, sem, *, n_pages):
    b = pl.program_id(0)
    slot = b & 1
    page = lat_hbm.shape[1]

    def copies(bb, sl, p):
        idx = pt_ref[bb * n_pages + p]
        off = p * page
        return (pltpu.make_async_copy(lat_hbm.at[idx], lat_buf.at[sl, pl.ds(off, page), :], sem.at[0, sl]),
                pltpu.make_async_copy(kpe_hbm.at[idx], kpe_buf.at[sl, :, pl.ds(off, page)], sem.at[1, sl]))

    def start_all(bb, sl):
        for p in range(n_pages):
            for cp in copies(bb, sl, p):
                cp.start()

    def wait_all(bb, sl):
        for p in range(n_pages):
            for cp in copies(bb, sl, p):
                cp.wait()

    @pl.when(b == 0)
    def _():
        start_all(0, 0)

    @pl.when(b + 1 < pl.num_programs(0))
    def _():
        start_all(b + 1, 1 - slot)

    wait_all(b, slot)
    q = q_ref[0]
    q_lat = q[:, :KV_LORA_RANK]
    q_pe = q[:, KV_LORA_RANK:KV_LORA_RANK + MLA_ROPE_DIM]
    chunk = n_pages * page // DECODE_CHUNKS
    parts = []
    for c in range(DECODE_CHUNKS):
        lat = lat_buf[slot, c * chunk:(c + 1) * chunk, :].astype(BF16)
        kpe_t = kpe_buf[slot, :, c * chunk:(c + 1) * chunk].astype(BF16)
        s = _dot_nt(q_lat, lat) + _dot(q_pe, kpe_t)
        mc = jnp.max(s, axis=-1, keepdims=True)
        p = jnp.exp(s - mc)
        parts.append((mc, jnp.sum(p, axis=-1, keepdims=True), _dot(p.astype(BF16), lat)))
    ks = self_ref[0].astype(F32)
    s_self = jnp.sum(q.astype(F32) * ks, axis=-1, keepdims=True)
    m = s_self
    for mc, _, _ in parts:
        m = jnp.maximum(m, mc)
    l = jnp.exp(s_self - m)
    acc = l * ks[:, :KV_LORA_RANK]
    for mc, lc, ac in parts:
        a = jnp.exp(mc - m)
        l = l + a * lc
        acc = acc + a * ac
    o_ref[0] = acc / l


def _mla_decode(page_table_flat, q, kself, cache_lat, cache_kpe_t):
    nb = q.shape[0]
    n_pages = page_table_flat.shape[0] // nb
    past = n_pages * cache_lat.shape[1]
    grid_spec = pltpu.PrefetchScalarGridSpec(
        num_scalar_prefetch=1,
        grid=(nb,),
        in_specs=[pl.BlockSpec((1, HEADS_PAD, QK_DIM), lambda b, pt: (b, 0, 0)),
                  pl.BlockSpec((1, 1, QK_DIM), lambda b, pt: (b, 0, 0)),
                  pl.BlockSpec(memory_space=pl.ANY),
                  pl.BlockSpec(memory_space=pl.ANY)],
        out_specs=pl.BlockSpec((1, HEADS_PAD, KV_LORA_RANK), lambda b, pt: (b, 0, 0)),
        scratch_shapes=[pltpu.VMEM((2, past, KV_LORA_RANK), F32), pltpu.VMEM((2, MLA_ROPE_DIM, past), F32),
                        pltpu.SemaphoreType.DMA((2, 2))],
    )
    return pl.pallas_call(
        functools.partial(_mla_decode_kernel, n_pages=n_pages),
        grid_spec=grid_spec,
        out_shape=jax.ShapeDtypeStruct((nb, HEADS_PAD, KV_LORA_RANK), F32),
        compiler_params=_params(("arbitrary",)),
        name="mla_decode",
    )(page_table_flat, q, kself, cache_lat, cache_kpe_t)


def _mla_up_kernel(o_ref, wuv_ref, g_ref, mix_ref):
    for hd in range(MLA_HEADS):
        a = hd * MLA_V_DIM
        oh = _dot(o_ref[hd], wuv_ref[hd])
        mix_ref[:, a:a + MLA_V_DIM] = (oh * g_ref[:, a:a + MLA_V_DIM].astype(F32)).astype(BF16)


def _mla_up(olat_t, wuv, gmla):
    nb = gmla.shape[0]
    return pl.pallas_call(
        _mla_up_kernel,
        out_shape=jax.ShapeDtypeStruct((nb, MLA_WIDTH), BF16),
        compiler_params=pltpu.CompilerParams(vmem_limit_bytes=VMEM_LIMIT),
        name="mla_up",
    )(olat_t, wuv, gmla)


def _sb_decode_kernel(pt_ref, q4_ref, g_ref, u_ref, ones_ref, k_hbm, v_hbm, o_ref,
                      kbuf, vbuf, sem, acc_sc, carry_sc, *, n_pages):
    b = pl.program_id(0)
    n_pairs = n_pages // 2

    def slot_of(bb, t):
        return jnp.where(t == 0, 2 + (bb & 1), t & 1)

    def copies(bb, t):
        slot = slot_of(bb, t)
        out = []
        for j in range(2):
            idx = pt_ref[bb * n_pages + (n_pages - 1 - j) - 2 * t]
            out.append(pltpu.make_async_copy(k_hbm.at[idx], kbuf.at[slot, j], sem.at[0, slot, j]))
            out.append(pltpu.make_async_copy(v_hbm.at[idx], vbuf.at[slot, j], sem.at[1, slot, j]))
        return out

    def start(bb, t):
        for cp in copies(bb, t):
            cp.start()

    def wait(bb, t):
        for cp in copies(bb, t):
            cp.wait()

    @pl.when(b == 0)
    def _():
        start(0, 0)

    @pl.when(b + 1 < pl.num_programs(0))
    def _():
        start(b + 1, 0)

    acc_sc[...] = jnp.zeros(acc_sc.shape, F32)
    carry_sc[...] = jnp.zeros(carry_sc.shape, F32)
    rows = kbuf.shape[2]
    own_lane = ((lax.broadcasted_iota(jnp.int32, (rows, LANE), 0) & (SB_HEADS - 1))
                == lax.broadcasted_iota(jnp.int32, (rows, LANE), 1))
    head_lane = lax.broadcasted_iota(jnp.int32, (1, LANE), 1) < SB_HEADS
    q4 = q4_ref[0]

    def body(state):
        t, _ = state
        slot = slot_of(b, t)
        wait(b, t)

        @pl.when(t + 1 < n_pairs)
        def _():
            start(b, t + 1)

        carry = carry_sc[...]
        contrib = None
        for j in range(2):
            z = _dot(kbuf[slot, j].astype(BF16), q4)
            lsm = jnp.where(own_lane, _neg_softplus(z), 0.0)
            hi, lo = _split_bf16(lsm)
            tail = _dot(u_ref[...], hi) + _dot(u_ref[...], lo) + carry
            w = jnp.where(own_lane, _sb_weights(z, lsm, tail), 0.0)
            wrow = _dot(w.astype(BF16), ones_ref[...])
            part = jnp.sum((wrow * vbuf[slot, j]).reshape(rows // 8, 8, LANE), axis=0)
            contrib = part if contrib is None else contrib + part
            carry = carry + jnp.sum(lsm, axis=0, keepdims=True)
        acc_sc[...] += contrib
        carry_sc[...] = carry
        live = jnp.max(jnp.where(head_lane, carry, -jnp.inf))
        return t + 1, (live < SB_EXIT).astype(jnp.int32)

    t_end, _ = lax.while_loop(lambda st: jnp.logical_and(st[0] < n_pairs, st[1] == 0), body,
                              (jnp.int32(0), jnp.int32(0)))

    @pl.when(t_end < n_pairs)
    def _():
        wait(b, t_end)

    acc = acc_sc[...]
    o = acc[:SB_HEADS] + acc[SB_HEADS:]
    o_ref[0] = (o * g_ref[0].astype(F32)).astype(BF16)


def _sb_decode(page_table_flat, q4, gsb, u, ones, cache_k, cache_v):
    nb = q4.shape[0]
    n_pages = page_table_flat.shape[0] // nb
    rows = cache_k.shape[1]
    grid_spec = pltpu.PrefetchScalarGridSpec(
        num_scalar_prefetch=1,
        grid=(nb,),
        in_specs=[pl.BlockSpec((1, SB_HEAD_DIM, LANE), lambda b, pt: (b, 0, 0)),
                  pl.BlockSpec((1, SB_HEADS, SB_HEAD_DIM), lambda b, pt: (b, 0, 0)),
                  pl.BlockSpec(u.shape, lambda b, pt: (0, 0)),
                  pl.BlockSpec(ones.shape, lambda b, pt: (0, 0)),
                  pl.BlockSpec(memory_space=pl.ANY),
                  pl.BlockSpec(memory_space=pl.ANY)],
        out_specs=pl.BlockSpec((1, SB_HEADS, SB_HEAD_DIM), lambda b, pt: (b, 0, 0)),
        scratch_shapes=[pltpu.VMEM((4, 2, rows, LANE), F32), pltpu.VMEM((4, 2, rows, LANE), F32),
                        pltpu.SemaphoreType.DMA((2, 4, 2)),
                        pltpu.VMEM((8, LANE), F32), pltpu.VMEM((1, LANE), F32)],
    )
    return pl.pallas_call(
        functools.partial(_sb_decode_kernel, n_pages=n_pages),
        grid_spec=grid_spec,
        out_shape=jax.ShapeDtypeStruct((nb, SB_HEADS, SB_HEAD_DIM), BF16),
        compiler_params=_params(("arbitrary",)),
        name="sb_decode",
    )(page_table_flat, q4, gsb, u, ones, cache_k, cache_v)


def _proj_out_kernel(mm_ref, ms_ref, wa_ref, wb_ref, g_ref, x_ref, y_ref):
    y = _dot(mm_ref[...], wa_ref[...]) + _dot(ms_ref[...], wb_ref[...])
    y_ref[...] = x_ref[...] + _rms(y, g_ref[...])


def _proj_out(mix_mla, mix_sb, wa, wb, gpost, x, *, tm):
    n = x.shape[0]

    def rows(width):
        return pl.BlockSpec((tm, width), lambda i: (i, 0))

    return pl.pallas_call(
        _proj_out_kernel,
        grid=(n // tm,),
        in_specs=[rows(MLA_WIDTH), rows(SB_WIDTH), _const_spec(wa.shape), _const_spec(wb.shape),
                  _const_spec(gpost.shape), rows(D_MODEL)],
        out_specs=rows(D_MODEL),
        out_shape=jax.ShapeDtypeStruct((n, D_MODEL), F32),
        compiler_params=_params(("parallel",)),
        name="proj_out",
    )(mix_mla, mix_sb, wa, wb, gpost, x)


def _prep_weights(w_in, w_q_b, w_kv_b, w_out):
    half = MLA_ROPE_DIM // 2
    swap = jnp.concatenate([jnp.arange(half, MLA_ROPE_DIM), jnp.arange(half)])
    kpe = w_in[:, _HEAD_COLS:_TAIL_START].astype(BF16)
    zpad = jnp.zeros((D_MODEL, ROPE_PAD - MLA_ROPE_DIM), BF16)
    win = (w_in[:, :_HEAD_COLS].astype(BF16),
           jnp.concatenate([kpe, zpad, kpe[:, swap], zpad], axis=1),
           w_in[:, _TAIL_START:].astype(BF16))

    wq = w_q_b.reshape(Q_LORA_RANK, MLA_HEADS, MLA_NOPE_DIM + MLA_ROPE_DIM)
    nope = wq[..., :MLA_NOPE_DIM].reshape(Q_LORA_RANK, MLA_HEADS * MLA_NOPE_DIM)
    rope = wq[..., MLA_NOPE_DIM:]
    rpad = jnp.zeros((Q_LORA_RANK, MLA_HEADS, ROPE_PAD - MLA_ROPE_DIM), w_q_b.dtype)
    rope_a = jnp.concatenate([rope, rpad], axis=-1).reshape(Q_LORA_RANK, MLA_HEADS * ROPE_PAD)
    rope_b = jnp.concatenate([rope[..., swap], rpad], axis=-1).reshape(Q_LORA_RANK, MLA_HEADS * ROPE_PAD)
    wq_ext = jnp.concatenate([nope, rope_a, rope_b], axis=1).astype(BF16)

    wuk = jnp.transpose(w_kv_b[..., :MLA_NOPE_DIM], (1, 2, 0)).astype(BF16)
    wuv = jnp.transpose(w_kv_b[..., MLA_NOPE_DIM:], (1, 0, 2)).astype(BF16)
    wa = w_out[:MLA_WIDTH].astype(BF16)
    wb = w_out[MLA_WIDTH:].astype(BF16)
    return win, wq_ext, wuk, wuv, wa, wb


def _rope_consts():
    half = MLA_ROPE_DIM // 2
    inv_freq = ROPE_BASE ** (-jnp.arange(half, dtype=F32) / half)
    invf = jnp.tile(inv_freq, ROPE_PAD // half)[None]
    sgn = jnp.tile(jnp.concatenate([-jnp.ones(half, F32), jnp.ones(half, F32)]), ROPE_PAD // MLA_ROPE_DIM)[None]
    return invf, sgn


def _suffix_matrix(n):
    j = lax.broadcasted_iota(jnp.int32, (n, n), 0)
    s = lax.broadcasted_iota(jnp.int32, (n, n), 1)
    return (j > s).astype(BF16)


def kernel(x_prompt, x_sample, cache_mla_latent, cache_mla_krope, cache_sb_k, cache_sb_v, page_table,
           w_in, g_q_a, w_q_b, g_kv_a, w_kv_b, w_out, g_pre, g_post):
    assert w_in.shape[0] == 1, "single layer"
    nb, seq, _ = x_prompt.shape
    dec_b, dec_s, _ = x_sample.shape
    assert dec_s == 1
    n_pool, page = cache_mla_latent.shape[1], cache_mla_latent.shape[2]
    n_pages = page_table.shape[1]
    past_len = n_pages * page

    win, wq_ext, wuk, wuv, wa, wb = _prep_weights(w_in[0], w_q_b[0], w_kv_b[0], w_out[0])
    invf, sgn = _rope_consts()
    gpre, gq, gkv, gpost = g_pre[0][None], g_q_a[0][None], g_kv_a[0][None], g_post[0][None]

    proj = functools.partial(_proj_in, gpre=gpre, win=win, gq=gq, gkv=gkv, wq=wq_ext, wuk=wuk,
                             invf=invf, sgn=sgn)

    (p_lat, p_kpe, p_kcat, p_gmla, p_sbq, p_sbk, p_sbv, p_sbkb, p_sbvb, p_gsb, p_qcat) = proj(
        x_prompt, tm=256, pos0=0, pos_stride=1)
    mix_mla = _mla_prompt(p_qcat, p_kcat, p_gmla, wuv, tq=256)
    sb_tq = 256
    u_row = jnp.concatenate([_suffix_matrix(sb_tq)] * 2, axis=0)
    mix_sb = _sb_prompt(p_sbq, p_sbkb, p_sbvb, p_gsb, u_row, tq=sb_tq)
    y_prompt = _proj_out(mix_mla.reshape(nb * seq, MLA_WIDTH), mix_sb.reshape(nb * seq, SB_WIDTH),
                         wa, wb, gpost, x_prompt.reshape(nb * seq, D_MODEL), tm=512).reshape(x_prompt.shape)

    xs = x_sample.reshape(1, dec_b, D_MODEL)
    (s_lat, s_kpe, s_kcat, s_gmla, s_sbq, s_sbk, s_sbv, _, _, s_gsb, s_qcat) = proj(
        xs, tm=dec_b, pos0=past_len, pos_stride=0)
    pt_flat = page_table.reshape(-1)
    q_dec = jnp.pad(jnp.transpose(s_qcat[0], (1, 0, 2)), ((0, 0), (0, HEADS_PAD - MLA_HEADS), (0, 0)))
    olat = _mla_decode(pt_flat, q_dec, s_kcat.reshape(dec_b, 1, QK_DIM),
                       cache_mla_latent[0], jnp.swapaxes(cache_mla_krope[0], 1, 2))
    olat_t = jnp.transpose(olat[:, :MLA_HEADS], (1, 0, 2)).astype(BF16)
    mix_mla_s = _mla_up(olat_t, wuv, s_gmla[0])

    q4 = jnp.pad(jnp.swapaxes(s_sbq[0].reshape(dec_b, SB_HEADS, SB_HEAD_DIM), 1, 2),
                 ((0, 0), (0, 0), (0, LANE - SB_HEADS)))
    rows = page * SB_HEADS
    u_col = _suffix_matrix(rows).T
    mix_sb_s = _sb_decode(pt_flat, q4, s_gsb.reshape(dec_b, SB_HEADS, SB_HEAD_DIM), u_col,
                          jnp.ones((LANE, LANE), BF16),
                          cache_sb_k.reshape(n_pool, rows, SB_HEAD_DIM),
                          cache_sb_v.reshape(n_pool, rows, SB_HEAD_DIM))
    y_sample = _proj_out(mix_mla_s, mix_sb_s.reshape(dec_b, SB_WIDTH), wa, wb, gpost,
                         x_sample.reshape(dec_b, D_MODEL), tm=dec_b).reshape(x_sample.shape)

    def heads(a):
        return a.reshape(*a.shape[:-1], SB_HEADS, SB_HEAD_DIM)

    return (y_prompt, y_sample,
            p_lat[None], p_kpe[None], heads(p_sbk)[None], heads(p_sbv)[None],
            s_lat.reshape(1, dec_b, 1, KV_LORA_RANK), s_kpe.reshape(1, dec_b, 1, MLA_ROPE_DIM),
            heads(s_sbk).reshape(1, dec_b, 1, SB_HEADS, SB_HEAD_DIM),
            heads(s_sbv).reshape(1, dec_b, 1, SB_HEADS, SB_HEAD_DIM))
```

```python
import functools
import math

import jax
import jax.numpy as jnp
from jax import lax
from jax.experimental import pallas as pl
from jax.experimental.pallas import tpu as pltpu

F32 = jnp.float32
BF16 = jnp.bfloat16

D_MODEL = 2048
MLA_HEADS = 12
MLA_NOPE_DIM = 128
MLA_ROPE_DIM = 64
MLA_V_DIM = 128
Q_LORA_RANK = 512
KV_LORA_RANK = 256
MLA_WIDTH = MLA_HEADS * MLA_V_DIM
SB_HEADS = 4
SB_HEAD_DIM = 128
SB_WIDTH = SB_HEADS * SB_HEAD_DIM
ROPE_BASE = 10000.0
EPS = 1e-6
MASK_VALUE = -1e30

LANE = 128
ROPE_PAD = LANE
QK_DIM = KV_LORA_RANK + ROPE_PAD
HEADS_PAD = 16
DECODE_CHUNKS = 2
VMEM_LIMIT = 56 * 1024 * 1024

_HEAD_COLS = Q_LORA_RANK + KV_LORA_RANK
_TAIL_START = _HEAD_COLS + MLA_ROPE_DIM
_T_GM = 0
_T_SQ = _T_GM + MLA_WIDTH
_T_SK = _T_SQ + SB_WIDTH
_T_SV = _T_SK + SB_WIDTH
_T_GS = _T_SV + SB_WIDTH
_TAIL_COLS = _T_GS + SB_WIDTH

SB_EXIT = -104.0

MLA_SCALE = 1.0 / math.sqrt(MLA_NOPE_DIM + MLA_ROPE_DIM)
SB_SCALE = 1.0 / math.sqrt(SB_HEAD_DIM)


def _params(sem):
    return pltpu.CompilerParams(dimension_semantics=sem, vmem_limit_bytes=VMEM_LIMIT)


def _const_spec(shape):
    nd = len(shape)
    return pl.BlockSpec(shape, lambda *_: (0,) * nd, pipeline_mode=pl.Buffered(1))


def _rms(x, g):
    return x * lax.rsqrt(jnp.mean(x * x, axis=-1, keepdims=True) + EPS) * g


def _silu(x):
    return x / (1.0 + jnp.exp(-x))


def _dot(a, b):
    return jnp.dot(a, b, preferred_element_type=F32)


def _dot_nt(a, b):
    return lax.dot_general(a, b, (((1,), (1,)), ((), ())), preferred_element_type=F32)


def _proj_in_kernel(x_ref, gpre_ref, whead_ref, wrope_ref, wtail_ref, gq_ref, gkv_ref, wq_ref, wuk_ref,
                    invf_ref, sgn_ref,
                    lat_ref, kpe_ref, kcat_ref, gmla_ref, sbq_ref, sbk_ref, sbv_ref,
                    sbkb_ref, sbvb_ref, gsb_ref, qcat_ref, *, tm, pos0, pos_stride):
    i = pl.program_id(1)
    h = _rms(x_ref[0], gpre_ref[...]).astype(BF16)

    def seg(w_ref, a, b):
        return _dot(h, w_ref[:, a:b])

    row = i * tm + lax.broadcasted_iota(jnp.int32, (tm, 1), 0)
    pos = (pos0 + pos_stride * row).astype(F32)
    ang = pos * invf_ref[...]
    cos = jnp.cos(ang)
    sin = jnp.sin(ang) * sgn_ref[...]

    lat = _rms(seg(whead_ref, Q_LORA_RANK, _HEAD_COLS), gkv_ref[...])
    lat_ref[0] = lat
    kr = seg(wrope_ref, 0, ROPE_PAD) * cos + seg(wrope_ref, ROPE_PAD, 2 * ROPE_PAD) * sin
    kpe_ref[0] = kr[:, :MLA_ROPE_DIM]
    kcat_ref[0, :, :KV_LORA_RANK] = lat.astype(BF16)
    kcat_ref[0, :, KV_LORA_RANK:] = kr.astype(BF16)

    gmla_ref[0] = _silu(seg(wtail_ref, _T_GM, _T_SQ)).astype(BF16)
    sbq_ref[0] = (seg(wtail_ref, _T_SQ, _T_SK) * SB_SCALE).astype(BF16)
    def store_heads(o_ref, val):
        for hd in range(SB_HEADS):
            o_ref[0, pl.ds(hd, tm, stride=SB_HEADS), :] = val[:, hd * SB_HEAD_DIM:(hd + 1) * SB_HEAD_DIM]

    k = seg(wtail_ref, _T_SK, _T_SV)
    store_heads(sbk_ref, k)
    sbkb_ref[0] = k.astype(BF16)
    v = seg(wtail_ref, _T_SV, _T_GS)
    store_heads(sbv_ref, v)
    sbvb_ref[0] = v.astype(BF16)
    gsb_ref[0] = _silu(seg(wtail_ref, _T_GS, _TAIL_COLS)).astype(BF16)

    cqn = _rms(seg(whead_ref, 0, Q_LORA_RANK), gq_ref[...]).astype(BF16)
    for hd in range(MLA_HEADS):
        a = hd * LANE
        qn = _dot(cqn, wq_ref[:, a:a + LANE]).astype(BF16)
        qa = _dot(qn, wuk_ref[hd]) * MLA_SCALE
        ra = MLA_HEADS * LANE + a
        rb = 2 * MLA_HEADS * LANE + a
        qp = (_dot(cqn, wq_ref[:, ra:ra + LANE]) * cos
              + _dot(cqn, wq_ref[:, rb:rb + LANE]) * sin) * MLA_SCALE
        qcat_ref[0, hd, :, :KV_LORA_RANK] = qa.astype(BF16)
        qcat_ref[0, hd, :, KV_LORA_RANK:] = qp.astype(BF16)


def _proj_in(x, gpre, win, gq, gkv, wq, wuk, invf, sgn, *, tm, pos0, pos_stride):
    whead, wrope, wtail = win
    nb, s, _ = x.shape
    grid = (nb, s // tm)

    def rows(width):
        return pl.BlockSpec((1, tm, width), lambda b, i: (b, i, 0))

    def out(width, dt):
        return jax.ShapeDtypeStruct((nb, s, width), dt)

    head_rows = pl.BlockSpec((1, tm * SB_HEADS, SB_HEAD_DIM), lambda b, i: (b, i, 0))
    head_out = jax.ShapeDtypeStruct((nb, s * SB_HEADS, SB_HEAD_DIM), F32)

    return pl.pallas_call(
        functools.partial(_proj_in_kernel, tm=tm, pos0=pos0, pos_stride=pos_stride),
        grid=grid,
        in_specs=[rows(D_MODEL), _const_spec(gpre.shape), _const_spec(whead.shape), _const_spec(wrope.shape),
                  _const_spec(wtail.shape), _const_spec(gq.shape),
                  _const_spec(gkv.shape), _const_spec(wq.shape), _const_spec(wuk.shape),
                  _const_spec(invf.shape), _const_spec(sgn.shape)],
        out_specs=[rows(KV_LORA_RANK), rows(MLA_ROPE_DIM), rows(QK_DIM), rows(MLA_WIDTH), rows(SB_WIDTH),
                   head_rows, head_rows, rows(SB_WIDTH), rows(SB_WIDTH), rows(SB_WIDTH),
                   pl.BlockSpec((1, MLA_HEADS, tm, QK_DIM), lambda b, i: (b, 0, i, 0))],
        out_shape=[out(KV_LORA_RANK, F32), out(MLA_ROPE_DIM, F32), out(QK_DIM, BF16), out(MLA_WIDTH, BF16),
                   out(SB_WIDTH, BF16), head_out, head_out, out(SB_WIDTH, BF16),
                   out(SB_WIDTH, BF16), out(SB_WIDTH, BF16),
                   jax.ShapeDtypeStruct((nb, MLA_HEADS, s, QK_DIM), BF16)],
        compiler_params=_params(("parallel", "arbitrary")),
        name="proj_in",
    )(x, gpre, whead, wrope, wtail, gq, gkv, wq, wuk, invf, sgn)


def _mla_prompt_kernel(q_ref, k_ref, g_ref, wuv_ref, o_ref, m_sc, l_sc, acc_sc, *, tq):
    i = pl.program_id(1)
    rows = MLA_HEADS * tq
    wide = 2 * tq
    q = q_ref[0].reshape(rows, QK_DIM)
    m_sc[...] = jnp.full(m_sc.shape, MASK_VALUE, F32)
    l_sc[...] = jnp.zeros(l_sc.shape, F32)
    acc_sc[...] = jnp.zeros(acc_sc.shape, F32)

    def lanes(x, width):
        return jnp.concatenate([x] * (width // LANE), axis=1)

    def step(start, width, q_off):
        kc = k_ref[0, pl.ds(start, width), :]
        s = _dot_nt(q, kc)
        if q_off is not None:
            qpos = q_off + lax.broadcasted_iota(jnp.int32, (tq, width), 0)
            kpos = lax.broadcasted_iota(jnp.int32, (tq, width), 1)
            s = jnp.where((kpos <= qpos)[None], s.reshape(MLA_HEADS, tq, width), MASK_VALUE).reshape(rows, width)
        m_prev = m_sc[...]
        m_new = jnp.maximum(m_prev, jnp.max(s, axis=-1, keepdims=True))
        alpha = jnp.exp(m_prev - m_new)
        p = jnp.exp(s - lanes(m_new, width))
        l_sc[...] = alpha * l_sc[...] + jnp.sum(p, axis=-1, keepdims=True)
        acc_sc[...] = lanes(alpha, KV_LORA_RANK) * acc_sc[...] + _dot(p.astype(BF16), kc[:, :KV_LORA_RANK])
        m_sc[...] = m_new

    def body(j, carry):
        step(pl.multiple_of(j * wide, wide), wide, None)
        return carry

    lax.fori_loop(0, i // 2, body, 0)

    @pl.when(i % 2 == 0)
    def _():
        step(pl.multiple_of(i * tq, tq), tq, 0)

    @pl.when(i % 2 == 1)
    def _():
        step(pl.multiple_of((i - 1) * tq, wide), wide, tq)

    o = (acc_sc[...] * lanes(1.0 / l_sc[...], KV_LORA_RANK)).astype(BF16)
    for hd in range(MLA_HEADS):
        oh = _dot(o[hd * tq:(hd + 1) * tq], wuv_ref[hd])
        a = hd * MLA_V_DIM
        o_ref[0, :, a:a + MLA_V_DIM] = (oh * g_ref[0, :, a:a + MLA_V_DIM].astype(F32)).astype(BF16)


def _mla_prompt(qcat, kcat, gmla, wuv, *, tq):
    nb, _, s, _ = qcat.shape
    rows = MLA_HEADS * tq
    return pl.pallas_call(
        functools.partial(_mla_prompt_kernel, tq=tq),
        grid=(nb, s // tq),
        in_specs=[pl.BlockSpec((1, MLA_HEADS, tq, QK_DIM), lambda b, i: (b, 0, i, 0)),
                  pl.BlockSpec((1, s, QK_DIM), lambda b, i: (b, 0, 0)),
                  pl.BlockSpec((1, tq, MLA_WIDTH), lambda b, i: (b, i, 0)),
                  _const_spec(wuv.shape)],
        out_specs=pl.BlockSpec((1, tq, MLA_WIDTH), lambda b, i: (b, i, 0)),
        out_shape=jax.ShapeDtypeStruct((nb, s, MLA_WIDTH), BF16),
        scratch_shapes=[pltpu.VMEM((rows, LANE), F32), pltpu.VMEM((rows, LANE), F32),
                        pltpu.VMEM((rows, KV_LORA_RANK), F32)],
        compiler_params=_params(("parallel", "arbitrary")),
        name="mla_prompt",
    )(qcat, kcat, gmla, wuv)


def _sb_weights(z, lsm, tail):
    return jnp.exp(z + lsm + tail)


def _neg_softplus(z):
    return -(jnp.maximum(z, 0.0) + jnp.log1p(jnp.exp(-jnp.abs(z))))


def _split_bf16(x):
    hi = x.astype(BF16)
    lo = (x - hi.astype(F32)).astype(BF16)
    return hi, lo


def _sb_prompt_kernel(q_ref, k_ref, v_ref, g_ref, u_ref, o_ref, acc_sc, carry_sc, *, tq):
    i = pl.program_id(1)
    acc_sc[...] = jnp.zeros(acc_sc.shape, F32)
    carry_sc[...] = jnp.zeros(carry_sc.shape, F32)
    below_diag = (lax.broadcasted_iota(jnp.int32, (tq, tq), 1)
                  < lax.broadcasted_iota(jnp.int32, (tq, tq), 0))

    def block(kb, diagonal):
        start = pl.multiple_of(kb * tq, tq)
        live = None
        for hd in range(SB_HEADS):
            cols = slice(hd * SB_HEAD_DIM, (hd + 1) * SB_HEAD_DIM)
            k = k_ref[0, pl.ds(start, tq), cols]
            v = v_ref[0, pl.ds(start, tq), cols]
            z = _dot_nt(q_ref[0, :, cols], k)
            lsm = _neg_softplus(z)
            if diagonal:
                lsm = jnp.where(below_diag, lsm, 0.0)
            hi, lo = _split_bf16(lsm)
            tail = _dot(jnp.concatenate([hi, lo], axis=1), u_ref[...]) + carry_sc[hd]
            w = _sb_weights(z, lsm, tail)
            if diagonal:
                w = jnp.where(below_diag, w, 0.0)
            acc_sc[hd] += _dot(w.astype(BF16), v)
            carry = carry_sc[hd] + jnp.sum(lsm, axis=-1, keepdims=True)
            carry_sc[hd] = carry
            top = jnp.max(carry)
            live = top if live is None else jnp.maximum(live, top)
        return live

    def body(state):
        kb, _ = state
        return kb - 1, (block(kb, False) < SB_EXIT).astype(jnp.int32)

    done = (block(i, True) < SB_EXIT).astype(jnp.int32)
    lax.while_loop(lambda st: jnp.logical_and(st[0] >= 0, st[1] == 0), body, (i - 1, done))
    for hd in range(SB_HEADS):
        cols = slice(hd * SB_HEAD_DIM, (hd + 1) * SB_HEAD_DIM)
        o_ref[0, :, cols] = (acc_sc[hd] * g_ref[0, :, cols].astype(F32)).astype(BF16)


def _sb_prompt(sbq, sbkb, sbvb, gsb, u2, *, tq):
    nb, s, _ = sbq.shape

    def qspec():
        return pl.BlockSpec((1, tq, SB_WIDTH), lambda b, i: (b, i, 0))

    def kspec():
        return pl.BlockSpec((1, s, SB_WIDTH), lambda b, i: (b, 0, 0))

    return pl.pallas_call(
        functools.partial(_sb_prompt_kernel, tq=tq),
        grid=(nb, s // tq),
        in_specs=[qspec(), kspec(), kspec(), qspec(), _const_spec(u2.shape)],
        out_specs=qspec(),
        out_shape=jax.ShapeDtypeStruct((nb, s, SB_WIDTH), BF16),
        scratch_shapes=[pltpu.VMEM((SB_HEADS, tq, SB_HEAD_DIM), F32), pltpu.VMEM((SB_HEADS, tq, 1), F32)],
        compiler_params=_params(("parallel", "arbitrary")),
        name="sb_prompt",
    )(sbq, sbkb, sbvb, gsb, u2)


def _mla_decode_kernel(pt_ref, q_ref, self_ref, lat_hbm, kpe_hbm, o_ref, lat_buf, kpe_buf, sem, *, n_pages):
    b = pl.program_id(0)
    slot = b & 1
    page = lat_hbm.shape[1]

    def copies(bb, sl, p):
        idx = pt_ref[bb * n_pages + p]
        off = p * page
        return (pltpu.make_async_copy(lat_hbm.at[idx], lat_buf.at[sl, pl.ds(off, page), :], sem.at[0, sl]),
                pltpu.make_async_copy(kpe_hbm.at[idx], kpe_buf.at[sl, :, pl.ds(off, page)], sem.at[1, sl]))

    def start_all(bb, sl):
        for p in range(n_pages):
            for cp in copies(bb, sl, p):
                cp.start()

    def wait_all(bb, sl):
        for p in range(n_pages):
            for cp in copies(bb, sl, p):
                cp.wait()

    @pl.when(b == 0)
    def _():
        start_all(0, 0)

    @pl.when(b + 1 < pl.num_programs(0))
    def _():
        start_all(b + 1, 1 - slot)

    wait_all(b, slot)
    q = q_ref[0]
    q_lat = q[:, :KV_LORA_RANK]
    q_pe = q[:, KV_LORA_RANK:KV_LORA_RANK + MLA_ROPE_DIM]
    chunk = n_pages * page // DECODE_CHUNKS
    parts = []
    for c in range(DECODE_CHUNKS):
        lat = lat_buf[slot, c * chunk:(c + 1) * chunk, :].astype(BF16)
        kpe_t = kpe_buf[slot, :, c * chunk:(c + 1) * chunk].astype(BF16)
        s = _dot_nt(q_lat, lat) + _dot(q_pe, kpe_t)
        mc = jnp.max(s, axis=-1, keepdims=True)
        p = jnp.exp(s - mc)
        parts.append((mc, jnp.sum(p, axis=-1, keepdims=True), _dot(p.astype(BF16), lat)))
    ks = self_ref[0].astype(F32)
    s_self = jnp.sum(q.astype(F32) * ks, axis=-1, keepdims=True)
    m = s_self
    for mc, _, _ in parts:
        m = jnp.maximum(m, mc)
    l = jnp.exp(s_self - m)
    acc = l * ks[:, :KV_LORA_RANK]
    for mc, lc, ac in parts:
        a = jnp.exp(mc - m)
        l = l + a * lc
        acc = acc + a * ac
    o_ref[0] = acc / l


def _mla_decode(page_table_flat, q, kself, cache_lat, cache_kpe_t):
    nb = q.shape[0]
    n_pages = page_table_flat.shape[0] // nb
    past = n_pages * cache_lat.shape[1]
    grid_spec = pltpu.PrefetchScalarGridSpec(
        num_scalar_prefetch=1,
        grid=(nb,),
        in_specs=[pl.BlockSpec((1, HEADS_PAD, QK_DIM), lambda b, pt: (b, 0, 0)),
                  pl.BlockSpec((1, 1, QK_DIM), lambda b, pt: (b, 0, 0)),
                  pl.BlockSpec(memory_space=pl.ANY),
                  pl.BlockSpec(memory_space=pl.ANY)],
        out_specs=pl.BlockSpec((1, HEADS_PAD, KV_LORA_RANK), lambda b, pt: (b, 0, 0)),
        scratch_shapes=[pltpu.VMEM((2, past, KV_LORA_RANK), F32), pltpu.VMEM((2, MLA_ROPE_DIM, past), F32),
                        pltpu.SemaphoreType.DMA((2, 2))],
    )
    return pl.pallas_call(
        functools.partial(_mla_decode_kernel, n_pages=n_pages),
        grid_spec=grid_spec,
        out_shape=jax.ShapeDtypeStruct((nb, HEADS_PAD, KV_LORA_RANK), F32),
        compiler_params=_params(("arbitrary",)),
        name="mla_decode",
    )(page_table_flat, q, kself, cache_lat, cache_kpe_t)


def _mla_up_kernel(o_ref, wuv_ref, g_ref, mix_ref):
    for hd in range(MLA_HEADS):
        a = hd * MLA_V_DIM
        oh = _dot(o_ref[hd], wuv_ref[hd])
        mix_ref[:, a:a + MLA_V_DIM] = (oh * g_ref[:, a:a + MLA_V_DIM].astype(F32)).astype(BF16)


def _mla_up(olat_t, wuv, gmla):
    nb = gmla.shape[0]
    return pl.pallas_call(
        _mla_up_kernel,
        out_shape=jax.ShapeDtypeStruct((nb, MLA_WIDTH), BF16),
        compiler_params=pltpu.CompilerParams(vmem_limit_bytes=VMEM_LIMIT),
        name="mla_up",
    )(olat_t, wuv, gmla)


def _sb_decode_kernel(pt_ref, q4_ref, g_ref, u_ref, ones_ref, k_hbm, v_hbm, o_ref,
                      kbuf, vbuf, sem, acc_sc, carry_sc, *, n_pages):
    b = pl.program_id(0)
    n_pairs = n_pages // 2

    def slot_of(bb, t):
        return jnp.where(t == 0, 2 + (bb & 1), t & 1)

    def copies(bb, t):
        slot = slot_of(bb, t)
        out = []
        for j in range(2):
            idx = pt_ref[bb * n_pages + (n_pages - 1 - j) - 2 * t]
            out.append(pltpu.make_async_copy(k_hbm.at[idx], kbuf.at[slot, j], sem.at[0, slot, j]))
            out.append(pltpu.make_async_copy(v_hbm.at[idx], vbuf.at[slot, j], sem.at[1, slot, j]))
        return out

    def start(bb, t):
        for cp in copies(bb, t):
            cp.start()

    def wait(bb, t):
        for cp in copies(bb, t):
            cp.wait()

    @pl.when(b == 0)
    def _():
        start(0, 0)

    @pl.when(b + 1 < pl.num_programs(0))
    def _():
        start(b + 1, 0)

    acc_sc[...] = jnp.zeros(acc_sc.shape, F32)
    carry_sc[...] = jnp.zeros(carry_sc.shape, F32)
    rows = kbuf.shape[2]
    own_lane = ((lax.broadcasted_iota(jnp.int32, (rows, LANE), 0) & (SB_HEADS - 1))
                == lax.broadcasted_iota(jnp.int32, (rows, LANE), 1))
    head_lane = lax.broadcasted_iota(jnp.int32, (1, LANE), 1) < SB_HEADS
    q4 = q4_ref[0]

    def body(state):
        t, _ = state
        slot = slot_of(b, t)
        wait(b, t)

        @pl.when(t + 1 < n_pairs)
        def _():
            start(b, t + 1)

        carry = carry_sc[...]
        contrib = None
        for j in range(2):
            z = _dot(kbuf[slot, j].astype(BF16), q4)
            lsm = jnp.where(own_lane, _neg_softplus(z), 0.0)
            hi, lo = _split_bf16(lsm)
            tail = _dot(u_ref[...], hi) + _dot(u_ref[...], lo) + carry
            w = jnp.where(own_lane, _sb_weights(z, lsm, tail), 0.0)
            wrow = _dot(w.astype(BF16), ones_ref[...])
            part = jnp.sum((wrow * vbuf[slot, j]).reshape(rows // 8, 8, LANE), axis=0)
            contrib = part if contrib is None else contrib + part
            carry = carry + jnp.sum(lsm, axis=0, keepdims=True)
        acc_sc[...] += contrib
        carry_sc[...] = carry
        live = jnp.max(jnp.where(head_lane, carry, -jnp.inf))
        return t + 1, (live < SB_EXIT).astype(jnp.int32)

    t_end, _ = lax.while_loop(lambda st: jnp.logical_and(st[0] < n_pairs, st[1] == 0), body,
                              (jnp.int32(0), jnp.int32(0)))

    @pl.when(t_end < n_pairs)
    def _():
        wait(b, t_end)

    acc = acc_sc[...]
    o = acc[:SB_HEADS] + acc[SB_HEADS:]
    o_ref[0] = (o * g_ref[0].astype(F32)).astype(BF16)


def _sb_decode(page_table_flat, q4, gsb, u, ones, cache_k, cache_v):
    nb = q4.shape[0]
    n_pages = page_table_flat.shape[0] // nb
    rows = cache_k.shape[1]
    grid_spec = pltpu.PrefetchScalarGridSpec(
        num_scalar_prefetch=1,
        grid=(nb,),
        in_specs=[pl.BlockSpec((1, SB_HEAD_DIM, LANE), lambda b, pt: (b, 0, 0)),
                  pl.BlockSpec((1, SB_HEADS, SB_HEAD_DIM), lambda b, pt: (b, 0, 0)),
                  pl.BlockSpec(u.shape, lambda b, pt: (0, 0)),
                  pl.BlockSpec(ones.shape, lambda b, pt: (0, 0)),
                  pl.BlockSpec(memory_space=pl.ANY),
                  pl.BlockSpec(memory_space=pl.ANY)],
        out_specs=pl.BlockSpec((1, SB_HEADS, SB_HEAD_DIM), lambda b, pt: (b, 0, 0)),
        scratch_shapes=[pltpu.VMEM((4, 2, rows, LANE), F32), pltpu.VMEM((4, 2, rows, LANE), F32),
                        pltpu.SemaphoreType.DMA((2, 4, 2)),
                        pltpu.VMEM((8, LANE), F32), pltpu.VMEM((1, LANE), F32)],
    )
    return pl.pallas_call(
        functools.partial(_sb_decode_kernel, n_pages=n_pages),
        grid_spec=grid_spec,
        out_shape=jax.ShapeDtypeStruct((nb, SB_HEADS, SB_HEAD_DIM), BF16),
        compiler_params=_params(("arbitrary",)),
        name="sb_decode",
    )(page_table_flat, q4, gsb, u, ones, cache_k, cache_v)


def _proj_out_kernel(mm_ref, ms_ref, wa_ref, wb_ref, g_ref, x_ref, y_ref):
    y = _dot(mm_ref[...], wa_ref[...]) + _dot(ms_ref[...], wb_ref[...])
    y_ref[...] = x_ref[...] + _rms(y, g_ref[...])


def _proj_out(mix_mla, mix_sb, wa, wb, gpost, x, *, tm):
    n = x.shape[0]

    def rows(width):
        return pl.BlockSpec((tm, width), lambda i: (i, 0))

    return pl.pallas_call(
        _proj_out_kernel,
        grid=(n // tm,),
        in_specs=[rows(MLA_WIDTH), rows(SB_WIDTH), _const_spec(wa.shape), _const_spec(wb.shape),
                  _const_spec(gpost.shape), rows(D_MODEL)],
        out_specs=rows(D_MODEL),
        out_shape=jax.ShapeDtypeStruct((n, D_MODEL), F32),
        compiler_params=_params(("parallel",)),
        name="proj_out",
    )(mix_mla, mix_sb, wa, wb, gpost, x)


def _prep_weights(w_in, w_q_b, w_kv_b, w_out):
    half = MLA_ROPE_DIM // 2
    swap = jnp.concatenate([jnp.arange(half, MLA_ROPE_DIM), jnp.arange(half)])
    kpe = w_in[:, _HEAD_COLS:_TAIL_START].astype(BF16)
    zpad = jnp.zeros((D_MODEL, ROPE_PAD - MLA_ROPE_DIM), BF16)
    win = (w_in[:, :_HEAD_COLS].astype(BF16),
           jnp.concatenate([kpe, zpad, kpe[:, swap], zpad], axis=1),
           w_in[:, _TAIL_START:].astype(BF16))

    wq = w_q_b.reshape(Q_LORA_RANK, MLA_HEADS, MLA_NOPE_DIM + MLA_ROPE_DIM)
    nope = wq[..., :MLA_NOPE_DIM].reshape(Q_LORA_RANK, MLA_HEADS * MLA_NOPE_DIM)
    rope = wq[..., MLA_NOPE_DIM:]
    rpad = jnp.zeros((Q_LORA_RANK, MLA_HEADS, ROPE_PAD - MLA_ROPE_DIM), w_q_b.dtype)
    rope_a = jnp.concatenate([rope, rpad], axis=-1).reshape(Q_LORA_RANK, MLA_HEADS * ROPE_PAD)
    rope_b = jnp.concatenate([rope[..., swap], rpad], axis=-1).reshape(Q_LORA_RANK, MLA_HEADS * ROPE_PAD)
    wq_ext = jnp.concatenate([nope, rope_a, rope_b], axis=1).astype(BF16)

    wuk = jnp.transpose(w_kv_b[..., :MLA_NOPE_DIM], (1, 2, 0)).astype(BF16)
    wuv = jnp.transpose(w_kv_b[..., MLA_NOPE_DIM:], (1, 0, 2)).astype(BF16)
    wa = w_out[:MLA_WIDTH].astype(BF16)
    wb = w_out[MLA_WIDTH:].astype(BF16)
    return win, wq_ext, wuk, wuv, wa, wb


def _rope_consts():
    half = MLA_ROPE_DIM // 2
    inv_freq = ROPE_BASE ** (-jnp.arange(half, dtype=F32) / half)
    invf = jnp.tile(inv_freq, ROPE_PAD // half)[None]
    sgn = jnp.tile(jnp.concatenate([-jnp.ones(half, F32), jnp.ones(half, F32)]), ROPE_PAD // MLA_ROPE_DIM)[None]
    return invf, sgn


def _suffix_matrix(n):
    j = lax.broadcasted_iota(jnp.int32, (n, n), 0)
    s = lax.broadcasted_iota(jnp.int32, (n, n), 1)
    return (j > s).astype(BF16)


def kernel(x_prompt, x_sample, cache_mla_latent, cache_mla_krope, cache_sb_k, cache_sb_v, page_table,
           w_in, g_q_a, w_q_b, g_kv_a, w_kv_b, w_out, g_pre, g_post):
    assert w_in.shape[0] == 1, "single layer"
    nb, seq, _ = x_prompt.shape
    dec_b, dec_s, _ = x_sample.shape
    assert dec_s == 1
    n_pool, page = cache_mla_latent.shape[1], cache_mla_latent.shape[2]
    n_pages = page_table.shape[1]
    past_len = n_pages * page

    win, wq_ext, wuk, wuv, wa, wb = _prep_weights(w_in[0], w_q_b[0], w_kv_b[0], w_out[0])
    invf, sgn = _rope_consts()
    gpre, gq, gkv, gpost = g_pre[0][None], g_q_a[0][None], g_kv_a[0][None], g_post[0][None]

    proj = functools.partial(_proj_in, gpre=gpre, win=win, gq=gq, gkv=gkv, wq=wq_ext, wuk=wuk,
                             invf=invf, sgn=sgn)

    (p_lat, p_kpe, p_kcat, p_gmla, p_sbq, p_sbk, p_sbv, p_sbkb, p_sbvb, p_gsb, p_qcat) = proj(
        x_prompt, tm=256, pos0=0, pos_stride=1)
    mix_mla = _mla_prompt(p_qcat, p_kcat, p_gmla, wuv, tq=256)
    sb_tq = 256
    u_row = jnp.concatenate([_suffix_matrix(sb_tq)] * 2, axis=0)
    mix_sb = _sb_prompt(p_sbq, p_sbkb, p_sbvb, p_gsb, u_row, tq=sb_tq)
    y_prompt = _proj_out(mix_mla.reshape(nb * seq, MLA_WIDTH), mix_sb.reshape(nb * seq, SB_WIDTH),
                         wa, wb, gpost, x_prompt.reshape(nb * seq, D_MODEL), tm=512).reshape(x_prompt.shape)

    xs = x_sample.reshape(1, dec_b, D_MODEL)
    (s_lat, s_kpe, s_kcat, s_gmla, s_sbq, s_sbk, s_sbv, _, _, s_gsb, s_qcat) = proj(
        xs, tm=dec_b, pos0=past_len, pos_stride=0)
    pt_flat = page_table.reshape(-1)
    q_dec = jnp.pad(jnp.transpose(s_qcat[0], (1, 0, 2)), ((0, 0), (0, HEADS_PAD - MLA_HEADS), (0, 0)))
    olat = _mla_decode(pt_flat, q_dec, s_kcat.reshape(dec_b, 1, QK_DIM),
                       cache_mla_latent[0], jnp.swapaxes(cache_mla_krope[0], 1, 2))
    olat_t = jnp.transpose(olat[:, :MLA_HEADS], (1, 0, 2)).astype(BF16)
    mix_mla_s = _mla_up(olat_t, wuv, s_gmla[0])

    q4 = jnp.pad(jnp.swapaxes(s_sbq[0].reshape(dec_b, SB_HEADS, SB_HEAD_DIM), 1, 2),
                 ((0, 0), (0, 0), (0, LANE - SB_HEADS)))
    rows = page * SB_HEADS
    u_col = _suffix_matrix(rows).T
    mix_sb_s = _sb_decode(pt_flat, q4, s_gsb.reshape(dec_b, SB_HEADS, SB_HEAD_DIM), u_col,
                          jnp.ones((LANE, LANE), BF16),
                          cache_sb_k.reshape(n_pool, rows, SB_HEAD_DIM),
                          cache_sb_v.reshape(n_pool, rows, SB_HEAD_DIM))
    y_sample = _proj_out(mix_mla_s, mix_sb_s.reshape(dec_b, SB_WIDTH), wa, wb, gpost,
                         x_sample.reshape(dec_b, D_MODEL), tm=dec_b).reshape(x_sample.shape)

    def heads(a, *lead):
        return a.reshape(*lead, SB_HEADS, SB_HEAD_DIM)

    return (y_prompt, y_sample,
            p_lat[None], p_kpe[None], heads(p_sbk, 1, nb, seq), heads(p_sbv, 1, nb, seq),
            s_lat.reshape(1, dec_b, 1, KV_LORA_RANK), s_kpe.reshape(1, dec_b, 1, MLA_ROPE_DIM),
            heads(s_sbk, 1, dec_b, 1), heads(s_sbv, 1, dec_b, 1))
```
